```python
import math
import jax, jax.numpy as jnp
from jax import lax
import numpy as np

D_MODEL = 1024
BATCH = 16
SEQ = 2048
DEPTH = 1

HEAD_DIM = 64
V_HEAD_DIM = 2 * HEAD_DIM
N_HEADS = D_MODEL // V_HEAD_DIM
QK_WIDTH = N_HEADS * 2 * HEAD_DIM
ATTN_WIDTH = N_HEADS * V_HEAD_DIM
Q_BLOCK = 128
CONV_WIDTH = D_MODEL
CONV_K = 3
PEER_HEADS = 8
N_KEYS = 128
N_EXPERTS = N_KEYS * N_KEYS
PEER_TOPK = 16
D_KEY = 256
D_HALF = D_KEY // 2
TOKEN_CHUNK = 128
EPS = 1e-6
SPLIT_SIZES = (QK_WIDTH, QK_WIDTH, ATTN_WIDTH, CONV_WIDTH, CONV_WIDTH, CONV_WIDTH, D_MODEL, D_MODEL)
SPLIT_POINTS = tuple(int(v) for v in np.cumsum(SPLIT_SIZES)[:-1])
IN_COLS = int(sum(SPLIT_SIZES))

kernel_name = "hybrid_diffattn_shortconv_peer"


def rmsnorm(x, g):
    xf = x.astype(jnp.float32)
    y = xf * lax.rsqrt(jnp.mean(xf * xf, axis=-1, keepdims=True) + EPS)
    return (y * g.astype(jnp.float32)).astype(x.dtype)


def alibi_slopes(n):
    return jnp.asarray(np.array([2.0 ** (-8.0 * (i + 1) / n) for i in range(n)], dtype=np.float32))


def diff_attention(q, k, v, lam, slopes):
    S = q.shape[1]
    scale = HEAD_DIM ** -0.5
    outs = []
    for i in range(S // Q_BLOCK):
        q0 = i * Q_BLOCK
        kv_len = q0 + Q_BLOCK
        qb = q[:, q0:kv_len]
        kb = k[:, :kv_len]
        vb = v[:, :kv_len]
        s = jnp.einsum('bqhpd,bkhpd->bhpqk', qb, kb).astype(jnp.float32) * scale
        dist = (q0 + jnp.arange(Q_BLOCK))[:, None] - jnp.arange(kv_len)[None, :]
        bias = -slopes[:, None, None, None] * dist.astype(jnp.float32)[None, None]
        s = jnp.where(dist >= 0, s + bias[None], -jnp.inf)
        p = jax.nn.softmax(s, axis=-1)
        a = p[:, :, 0] - lam * p[:, :, 1]
        outs.append(jnp.einsum('bhqk,bkhe->bqhe', a.astype(v.dtype), vb))
    return jnp.concatenate(outs, axis=1)


def short_conv(gate_b, gate_c, xin, conv_w):
    y = gate_c * xin
    z = lax.conv_general_dilated(
        y, conv_w.astype(y.dtype)[:, None, :], window_strides=(1,),
        padding=[(CONV_K - 1, 0)], dimension_numbers=('NWC', 'WIO', 'NWC'),
        feature_group_count=CONV_WIDTH)
    return gate_b * z


def peer(h, w_query, sub_keys, u, v):
    B, S, D = h.shape
    hc_all = h.reshape((B * S) // TOKEN_CHUNK, TOKEN_CHUNK, D)

    def chunk(hc):
        q = (hc @ w_query).reshape(TOKEN_CHUNK, PEER_HEADS, 2, D_HALF)
        s = jnp.einsum('thpd,hpkd->thpk', q, sub_keys)
        sv, si = lax.top_k(s, PEER_TOPK)
        cand = (sv[:, :, 0, :, None] + sv[:, :, 1, None, :]).reshape(TOKEN_CHUNK, PEER_HEADS, PEER_TOPK * PEER_TOPK)
        cidx = (si[:, :, 0, :, None] * N_KEYS + si[:, :, 1, None, :]).reshape(TOKEN_CHUNK, PEER_HEADS, PEER_TOPK * PEER_TOPK)
        tv, tp = lax.top_k(cand, PEER_TOPK)
        eidx = jnp.take_along_axis(cidx, tp, axis=-1)
        g = jax.nn.softmax(tv.astype(jnp.float32), axis=-1)
        ue = jnp.take(u, eidx, axis=0)
        a = jnp.einsum('thkd,td->thk', ue, hc)
        w = (g * jax.nn.gelu(a.astype(jnp.float32), approximate=False)).astype(hc.dtype)
        ve = jnp.take(v, eidx, axis=0)
        return jnp.einsum('thk,thkd->td', w, ve)

    return lax.map(chunk, hc_all).reshape(B, S, D)


def setup_inputs(seed: int = 0) -> dict:
    key = jax.random.key(seed)
    ks = jax.random.split(key, 20)
    f32 = jnp.float32
    sd = D_MODEL ** -0.5
    nrm = lambda k, shape, s: jax.random.normal(k, shape, f32) * s
    return {
        'x': nrm(ks[0], (BATCH, SEQ, D_MODEL), 1.0),
        'norm_mix_g': 1.0 + nrm(ks[1], (DEPTH, D_MODEL), 0.02),
        'w_in': nrm(ks[2], (DEPTH, D_MODEL, IN_COLS), sd),
        'q_norm_g': 1.0 + nrm(ks[3], (DEPTH, HEAD_DIM), 0.02),
        'k_norm_g': 1.0 + nrm(ks[4], (DEPTH, HEAD_DIM), 0.02),
        'lambda_q1': nrm(ks[5], (DEPTH, HEAD_DIM), 0.1),
        'lambda_k1': nrm(ks[6], (DEPTH, HEAD_DIM), 0.1),
        'lambda_q2': nrm(ks[7], (DEPTH, HEAD_DIM), 0.1),
        'lambda_k2': nrm(ks[8], (DEPTH, HEAD_DIM), 0.1),
        'subln_g': 1.0 + nrm(ks[9], (DEPTH, V_HEAD_DIM), 0.02),
        'w_attn_proj': nrm(ks[10], (DEPTH, ATTN_WIDTH, D_MODEL), ATTN_WIDTH ** -0.5),
        'conv_w': nrm(ks[11], (DEPTH, CONV_K, CONV_WIDTH), CONV_K ** -0.5),
        'w_conv_proj': nrm(ks[12], (DEPTH, CONV_WIDTH, D_MODEL), CONV_WIDTH ** -0.5),
        'w_out': nrm(ks[13], (DEPTH, D_MODEL, D_MODEL), sd),
        'norm_ffn_g': 1.0 + nrm(ks[14], (DEPTH, D_MODEL), 0.02),
        'peer_w_query': nrm(ks[15], (DEPTH, D_MODEL, PEER_HEADS * D_KEY), sd),
        'peer_sub_keys': nrm(ks[16], (DEPTH, PEER_HEADS, 2, N_KEYS, D_HALF), D_HALF ** -0.5),
        'peer_u': nrm(ks[17], (DEPTH, N_EXPERTS, D_MODEL), sd),
        'peer_v': nrm(ks[18], (DEPTH, N_EXPERTS, D_MODEL), (PEER_HEADS * PEER_TOPK) ** -0.5),
    }


def reference(x, norm_mix_g, w_in, q_norm_g, k_norm_g, lambda_q1, lambda_k1, lambda_q2, lambda_k2,
              subln_g, w_attn_proj, conv_w, w_conv_proj, w_out, norm_ffn_g,
              peer_w_query, peer_sub_keys, peer_u, peer_v):
    B, S, _ = x.shape
    slopes = alibi_slopes(N_HEADS)
    for l in range(DEPTH):
        lam_init = 0.8 - 0.6 * math.exp(-0.3 * l)
        h = rmsnorm(x, norm_mix_g[l])
        proj = h @ w_in[l]
        q, k, v, cb, cc, cx, ga, gc = jnp.split(proj, SPLIT_POINTS, axis=-1)
        q = rmsnorm(q.reshape(B, S, N_HEADS, 2, HEAD_DIM), q_norm_g[l])
        k = rmsnorm(k.reshape(B, S, N_HEADS, 2, HEAD_DIM), k_norm_g[l])
        v = v.reshape(B, S, N_HEADS, V_HEAD_DIM)
        lam = (jnp.exp(jnp.sum(lambda_q1[l].astype(jnp.float32) * lambda_k1[l].astype(jnp.float32)))
               - jnp.exp(jnp.sum(lambda_q2[l].astype(jnp.float32) * lambda_k2[l].astype(jnp.float32)))
               + lam_init)
        attn = diff_attention(q, k, v, lam, slopes)
        attn = rmsnorm(attn, subln_g[l]) * (1.0 - lam_init)
        attn = attn.reshape(B, S, ATTN_WIDTH) @ w_attn_proj[l]
        conv = short_conv(cb, cc, cx, conv_w[l]) @ w_conv_proj[l]
        mixed = jax.nn.sigmoid(ga) * attn + jax.nn.sigmoid(gc) * conv
        x = x + mixed @ w_out[l]
        h2 = rmsnorm(x, norm_ffn_g[l])
        x = x + peer(h2, peer_w_query[l], peer_sub_keys[l], peer_u[l], peer_v[l])
    return x
```

```python
import functools
import math

import numpy as np
import jax
import jax.numpy as jnp
from jax import lax
from jax.experimental import pallas as pl
from jax.experimental.pallas import tpu as pltpu

F32 = jnp.float32
BF16 = jnp.bfloat16
I32 = jnp.int32

EPS = 1e-6
HEAD_DIM = 64
V_HEAD_DIM = 2 * HEAD_DIM
CONV_K = 3
PEER_HEADS = 8
N_KEYS = 128
PEER_TOPK = 16
D_HALF = 128
LANES = 128
SUBLANES = 8
MXU_COL = 256
NEG = -1e30
VMEM_LIMIT = 56 * 1024 * 1024

_NT = (((1,), (1,)), ((), ()))


def _cparams(n_axes, vmem=None):
    return pltpu.CompilerParams(
        dimension_semantics=("arbitrary",) * n_axes,
        vmem_limit_bytes=vmem)


def _const_spec(shape):
    nd = len(shape)
    return pl.BlockSpec(shape, lambda *_: (0,) * nd, pipeline_mode=pl.Buffered(1))


def _rmsnorm_kernel(x_ref, g_ref, o_ref):
    x = x_ref[...]
    ms = jnp.mean(x * x, axis=-1, keepdims=True)
    o_ref[...] = ((x * lax.rsqrt(ms + EPS)) * g_ref[...]).astype(o_ref.dtype)


def _rmsnorm(x2d, g, tm):
    n, d = x2d.shape
    return pl.pallas_call(
        _rmsnorm_kernel,
        grid=(n // tm,),
        in_specs=[pl.BlockSpec((tm, d), lambda i: (i, 0)), _const_spec((1, d))],
        out_specs=pl.BlockSpec((tm, d), lambda i: (i, 0)),
        out_shape=jax.ShapeDtypeStruct((n, d), BF16),
        compiler_params=_cparams(1),
        name="rmsnorm_in",
    )(x2d, g.reshape(1, d))


def _qk_kernel(h_ref, w_ref, g_ref, bd_ref, o_ref):
    p = jnp.dot(h_ref[...], w_ref[...], preferred_element_type=F32)
    sq = p * p
    hi = sq.astype(BF16)
    lo = (sq - hi.astype(F32)).astype(BF16)
    bd = bd_ref[...]
    parts = []
    for c in range(p.shape[1] // MXU_COL):
        sl = slice(c * MXU_COL, (c + 1) * MXU_COL)
        parts.append(jnp.dot(hi[:, sl], bd, preferred_element_type=F32)
                     + jnp.dot(lo[:, sl], bd, preferred_element_type=F32))
    ss = jnp.concatenate(parts, axis=1)
    y = p * lax.rsqrt(ss * (1.0 / HEAD_DIM) + EPS)
    o_ref[...] = (y * g_ref[...]).astype(o_ref.dtype)


def _qk_proj(h2d, w_qk, gains, tm):
    n, d = h2d.shape
    width = w_qk.shape[2]
    grp = np.arange(MXU_COL) // HEAD_DIM
    bd = jnp.asarray((grp[:, None] == grp[None, :]).astype(np.float32), BF16)
    return pl.pallas_call(
        _qk_kernel,
        grid=(2, n // tm),
        in_specs=[
            pl.BlockSpec((tm, d), lambda g, i: (i, 0)),
            pl.BlockSpec((None, d, width), lambda g, i: (g, 0, 0)),
            pl.BlockSpec((None, 1, width), lambda g, i: (g, 0, 0)),
            _const_spec((MXU_COL, MXU_COL)),
        ],
        out_specs=pl.BlockSpec((None, tm, width), lambda g, i: (g, i, 0)),
        out_shape=jax.ShapeDtypeStruct((2, n, width), BF16),
        compiler_params=_cparams(2, VMEM_LIMIT),
        name="qk_proj",
    )(h2d, w_qk, gains, bd)


def _vt_kernel(h_ref, wt_ref, o_ref):
    o_ref[...] = lax.dot_general(wt_ref[...], h_ref[...], _NT,
                                 preferred_element_type=F32).astype(o_ref.dtype)


def _vt_proj(h3d, w_vt, tm):
    b, s, d = h3d.shape
    dout = w_vt.shape[0]
    return pl.pallas_call(
        _vt_kernel,
        grid=(b, s // tm),
        in_specs=[pl.BlockSpec((None, tm, d), lambda i, j: (i, j, 0)),
                  _const_spec((dout, d))],
        out_specs=pl.BlockSpec((None, dout, tm), lambda i, j: (i, 0, j)),
        out_shape=jax.ShapeDtypeStruct((b, dout, s), BF16),
        compiler_params=_cparams(2, VMEM_LIMIT),
        name="vt_proj",
    )(h3d, w_vt)


def _gate_kernel(h_ref, w_ref, o_ref):
    p = jnp.dot(h_ref[...], w_ref[...], preferred_element_type=F32)
    o_ref[...] = (1.0 / (1.0 + jnp.exp(-p))).astype(o_ref.dtype)


def _gate_proj(h2d, w_g, tm):
    n, d = h2d.shape
    width = w_g.shape[2]
    return pl.pallas_call(
        _gate_kernel,
        grid=(2, n // tm),
        in_specs=[pl.BlockSpec((tm, d), lambda g, i: (i, 0)),
                  pl.BlockSpec((None, d, width), lambda g, i: (g, 0, 0))],
        out_specs=pl.BlockSpec((None, tm, width), lambda g, i: (g, i, 0)),
        out_shape=jax.ShapeDtypeStruct((2, n, width), BF16),
        compiler_params=_cparams(2, VMEM_LIMIT),
        name="gate_proj",
    )(h2d, w_g)


def _conv_kernel(h_ref, w_ref, cw_ref, o_ref, carry_ref):
    @pl.when(pl.program_id(1) == 0)
    def _():
        carry_ref[...] = jnp.zeros_like(carry_ref)

    width = o_ref.shape[1]
    tm = o_ref.shape[0]
    p = jnp.dot(h_ref[...], w_ref[...], preferred_element_type=F32)
    y = p[:, width:2 * width] * p[:, 2 * width:]
    rows = lax.broadcasted_iota(I32, y.shape, 0)
    prev2 = carry_ref[SUBLANES - 2:SUBLANES - 1, :]
    prev1 = carry_ref[SUBLANES - 1:SUBLANES, :]
    y1 = jnp.where(rows == 0, prev1, pltpu.roll(y, 1, 0))
    y2 = jnp.where(rows == 0, prev2, jnp.where(rows == 1, prev1, pltpu.roll(y, 2, 0)))
    cw = cw_ref[...]
    z = cw[0:1, :] * y2 + cw[1:2, :] * y1 + cw[2:3, :] * y
    o_ref[...] = (p[:, :width] * z).astype(o_ref.dtype)
    carry_ref[...] = y[tm - SUBLANES:, :]


def _conv_branch(h3d, w_c, conv_w, tm):
    b, s, d = h3d.shape
    width = conv_w.shape[1]
    return pl.pallas_call(
        _conv_kernel,
        grid=(b, s // tm),
        in_specs=[pl.BlockSpec((None, tm, d), lambda i, j: (i, j, 0)),
                  _const_spec((d, 3 * width)),
                  _const_spec((CONV_K, width))],
        out_specs=pl.BlockSpec((None, tm, width), lambda i, j: (i, j, 0)),
        out_shape=jax.ShapeDtypeStruct((b, s, width), BF16),
        scratch_shapes=[pltpu.VMEM((SUBLANES, width), F32)],
        compiler_params=_cparams(2, VMEM_LIMIT),
        name="conv_branch",
    )(h3d, w_c, conv_w)


def _attn_kernel(slopes_ref, q_ref, k_ref, vt_ref, lq1_ref, lk1_ref, lq2_ref, lk2_ref, sg_ref,
                 o_ref, kaug_ref, qaug_ref, m_ref, l_ref, acc_ref, *, tq, lam_init):
    h = pl.program_id(1)
    qi = pl.program_id(2)
    s_len = k_ref.shape[0]
    slope = slopes_ref[h]

    @pl.when(qi == 0)
    def _():
        kpos = lax.broadcasted_iota(I32, (s_len, LANES), 0).astype(F32) * slope
        lane = lax.broadcasted_iota(I32, (s_len, LANES), 1)
        b_hi = kpos.astype(BF16).astype(F32)
        b_lo = kpos - b_hi
        extra = jnp.where(lane == 0, b_hi, jnp.where(lane == 1, b_lo, jnp.where(lane < 4, 1.0, 0.0)))
        kaug_ref[:, :LANES] = k_ref[...]
        kaug_ref[:, LANES:] = extra.astype(BF16)

    q = q_ref[...].astype(F32)
    lane = lax.broadcasted_iota(I32, (tq, LANES), 1)
    c_full = -((jnp.zeros((tq, LANES), I32) + qi * tq).astype(F32) * slope)
    c_hi = c_full.astype(BF16).astype(F32)
    c_lo = c_full - c_hi
    qextra = jnp.where(lane < 2, 1.0, jnp.where(lane == 2, c_hi, jnp.where(lane == 3, c_lo, 0.0)))
    qextra = qextra.astype(BF16)
    qaug_ref[:tq, :LANES] = jnp.where(lane < HEAD_DIM, q, 0.0).astype(BF16)
    qaug_ref[tq:, :LANES] = jnp.where(lane >= HEAD_DIM, q, 0.0).astype(BF16)
    qaug_ref[:tq, LANES:] = qextra
    qaug_ref[tq:, LANES:] = qextra

    m_ref[...] = jnp.full(m_ref.shape, NEG, F32)
    l_ref[...] = jnp.zeros(l_ref.shape, F32)
    acc_ref[...] = jnp.zeros(acc_ref.shape, F32)

    def step(kt, masked):
        k0 = pl.multiple_of(kt * tq, tq)
        kt_aug = kaug_ref[pl.ds(k0, tq), :]
        s = lax.dot_general(kt_aug, qaug_ref[...], _NT, preferred_element_type=F32)
        if masked:
            kk = lax.broadcasted_iota(I32, s.shape, 0)
            qq = lax.broadcasted_iota(I32, s.shape, 1)
            qq = jnp.where(qq >= tq, qq - tq, qq)
            s = jnp.where(kk <= qq, s, NEG)
        m_old = m_ref[...]
        m_new = jnp.maximum(m_old, jnp.max(s, axis=0, keepdims=True))
        alpha = jnp.exp(m_old - m_new)
        p = jnp.exp(s - m_new)
        l_ref[...] = alpha * l_ref[...] + jnp.sum(p, axis=0, keepdims=True)
        pv = jnp.dot(vt_ref[:, pl.ds(k0, tq)], p.astype(BF16), preferred_element_type=F32)
        acc_ref[...] = alpha * acc_ref[...] + pv
        m_ref[...] = m_new

    def body(kt, carry):
        step(kt, False)
        return carry

    lax.fori_loop(0, qi, body, 0)
    step(qi, True)

    lam = (jnp.exp(jnp.sum(lq1_ref[...] * lk1_ref[...], axis=1, keepdims=True))
           - jnp.exp(jnp.sum(lq2_ref[...] * lk2_ref[...], axis=1, keepdims=True))
           + lam_init)
    o = acc_ref[...] / l_ref[...]
    a = o[:, :tq] - lam * o[:, tq:]
    ms = jnp.mean(a * a, axis=0, keepdims=True)
    a = (a * lax.rsqrt(ms + EPS)) * sg_ref[...]
    a = a * (1.0 - lam_init)
    o_ref[...] = a.T.astype(o_ref.dtype)


def _diff_attention(q3d, k3d, vt3d, slopes, lq1, lk1, lq2, lk2, subln_g, lam_init, tq):
    b, s, width = q3d.shape
    n_heads = width // V_HEAD_DIM
    lam_spec = pl.BlockSpec((1, HEAD_DIM), lambda i, h, j, *_: (0, 0))
    grid_spec = pltpu.PrefetchScalarGridSpec(
        num_scalar_prefetch=1,
        grid=(b, n_heads, s // tq),
        in_specs=[
            pl.BlockSpec((None, tq, V_HEAD_DIM), lambda i, h, j, *_: (i, j, h)),
            pl.BlockSpec((None, s, V_HEAD_DIM), lambda i, h, j, *_: (i, 0, h)),
            pl.BlockSpec((None, V_HEAD_DIM, s), lambda i, h, j, *_: (i, h, 0)),
            lam_spec, lam_spec, lam_spec, lam_spec,
            pl.BlockSpec((V_HEAD_DIM, 1), lambda i, h, j, *_: (0, 0)),
        ],
        out_specs=pl.BlockSpec((None, tq, V_HEAD_DIM), lambda i, h, j, *_: (i, j, h)),
        scratch_shapes=[
            pltpu.VMEM((s, 2 * LANES), BF16),
            pltpu.VMEM((2 * tq, 2 * LANES), BF16),
            pltpu.VMEM((1, 2 * tq), F32),
            pltpu.VMEM((1, 2 * tq), F32),
            pltpu.VMEM((V_HEAD_DIM, 2 * tq), F32),
        ],
    )
    return pl.pallas_call(
        functools.partial(_attn_kernel, tq=tq, lam_init=lam_init),
        grid_spec=grid_spec,
        out_shape=jax.ShapeDtypeStruct((b, s, width), BF16),
        compiler_params=_cparams(3, VMEM_LIMIT),
        name="diff_attention",
    )(slopes, q3d, k3d, vt3d, lq1.reshape(1, -1), lk1.reshape(1, -1), lq2.reshape(1, -1),
      lk2.reshape(1, -1), subln_g.reshape(-1, 1))


def _mix_kernel(x_ref, attn_ref, conv_ref, sga_ref, sgc_ref, wa_ref, wc_ref, wo_ref, gf_ref, wq_ref,
                x1_ref, h2_ref, qp_ref):
    a = jnp.dot(attn_ref[...], wa_ref[...], preferred_element_type=F32)
    c = jnp.dot(conv_ref[...], wc_ref[...], preferred_element_type=F32)
    mixed = sga_ref[...].astype(F32) * a + sgc_ref[...].astype(F32) * c
    x1 = x_ref[...] + jnp.dot(mixed.astype(BF16), wo_ref[...], preferred_element_type=F32)
    x1_ref[...] = x1
    ms = jnp.mean(x1 * x1, axis=-1, keepdims=True)
    h2 = (x1 * lax.rsqrt(ms + EPS)) * gf_ref[...]
    h2_ref[...] = h2
    qp_ref[...] = jnp.dot(h2.astype(BF16), wq_ref[...], preferred_element_type=F32).astype(qp_ref.dtype)


def _mix(x2d, attn2d, conv2d, sg, wa, wc, wo, gf, wq, tm):
    n, d = x2d.shape
    qw = wq.shape[1]
    row = lambda width: pl.BlockSpec((tm, width), lambda i: (i, 0))
    return pl.pallas_call(
        _mix_kernel,
        grid=(n // tm,),
        in_specs=[row(d), row(d), row(d),
                  pl.BlockSpec((None, tm, d), lambda i: (0, i, 0)),
                  pl.BlockSpec((None, tm, d), lambda i: (1, i, 0)),
                  _const_spec((d, d)), _const_spec((d, d)), _const_spec((d, d)),
                  _const_spec((1, d)), _const_spec((d, qw))],
        out_specs=[row(d), row(d), row(qw)],
        out_shape=[jax.ShapeDtypeStruct((n, d), F32),
                   jax.ShapeDtypeStruct((n, d), F32),
                   jax.ShapeDtypeStruct((n, qw), BF16)],
        compiler_params=_cparams(1, VMEM_LIMIT),
        name="mix_residual",
    )(x2d, attn2d, conv2d, sg, sg, wa, wc, wo, gf.reshape(1, d), wq)


def _top_rounds(c, rowid, payload, n_rounds):
    vals, rids, pays = [], [], []
    for _ in range(n_rounds):
        m = jnp.max(c, axis=0, keepdims=True)
        rid = jnp.min(jnp.where(c == m, rowid, 1e9), axis=0, keepdims=True)
        hit = rowid == rid
        vals.append(m)
        rids.append(rid)
        if payload is not None:
            pays.append(jnp.sum(jnp.where(hit, payload, 0.0), axis=0, keepdims=True))
        c = jnp.where(hit, NEG, c)
    return vals, rids, pays


def _stack_rows(rows):
    n, tt = len(rows), rows[0].shape[1]
    rid = lax.broadcasted_iota(I32, (n, tt), 0)
    out = jnp.zeros((n, tt), rows[0].dtype)
    for r, row in enumerate(rows):
        out = jnp.where(rid == r, row, out)
    return out


def _topk_kernel(qp_ref, sk_ref, eidx_ref, g_ref):
    tt = qp_ref.shape[0]
    key_id = lax.broadcasted_iota(I32, (N_KEYS, tt), 0).astype(F32)
    cand_id = lax.broadcasted_iota(I32, (PEER_TOPK * PEER_TOPK, tt), 0).astype(F32)
    e_rows, g_rows = [], []
    for h in range(PEER_HEADS):
        sv, si = [], []
        for p in range(2):
            hp = 2 * h + p
            q = qp_ref[:, hp * D_HALF:(hp + 1) * D_HALF]
            s = lax.dot_general(sk_ref[hp], q, _NT, preferred_element_type=F32)
            vals, rids, _ = _top_rounds(s, key_id, None, PEER_TOPK)
            sv.append(_stack_rows(vals))
            si.append(_stack_rows(rids))
        cand = jnp.concatenate([sv[0][a:a + 1, :] + sv[1] for a in range(PEER_TOPK)], axis=0)
        cidx = jnp.concatenate([si[0][a:a + 1, :] * float(N_KEYS) + si[1] for a in range(PEER_TOPK)],
                               axis=0)
        tv, _, te = _top_rounds(cand, cand_id, cidx, PEER_TOPK)
        tv = _stack_rows(tv)
        ex = jnp.exp(tv - tv[0:1, :])
        g_rows.append(ex / jnp.sum(ex, axis=0, keepdims=True))
        e_rows.append(_stack_rows(te))
    eidx_ref[...] = jnp.concatenate(e_rows, axis=0).T.astype(I32)
    g_ref[...] = jnp.concatenate(g_rows, axis=0).T


def _peer_topk(qp, sub_keys, tt):
    n, qw = qp.shape
    nk = PEER_HEADS * PEER_TOPK
    return pl.pallas_call(
        _topk_kernel,
        grid=(n // tt,),
        in_specs=[pl.BlockSpec((tt, qw), lambda i: (i, 0)),
                  _const_spec(sub_keys.shape)],
        out_specs=[pl.BlockSpec((tt, nk), lambda i: (i, 0)),
                   pl.BlockSpec((tt, nk), lambda i: (i, 0))],
        out_shape=[jax.ShapeDtypeStruct((n, nk), I32),
                   jax.ShapeDtypeStruct((n, nk), F32)],
        compiler_params=_cparams(1, VMEM_LIMIT),
        name="peer_topk",
    )(qp, sub_keys)


def _pack_table(t):
    e, d = t.shape
    tb = t.astype(BF16).reshape(e, d // (2 * LANES), 2, LANES)
    tb = jnp.swapaxes(tb, 2, 3)
    return lax.bitcast_convert_type(tb, I32)


def _gather_rows(eidx_ref, t, tab_ref, gbuf_ref, n_sel, words):
    for k in range(n_sel):
        e = eidx_ref[t, k]
        gbuf_ref[pl.ds(k * words, words), :] = tab_ref[e]


def _diag_mask(ds, n_sel):
    col = lax.broadcasted_iota(I32, (ds, ds * n_sel), 1)
    row = lax.broadcasted_iota(I32, (ds, ds * n_sel), 0)
    return (col % ds) == row


def _gelu(a):
    return 0.5 * a * (1.0 + lax.erf(a * (1.0 / math.sqrt(2.0))))


def _peer_u_kernel(eidx_ref, h_ref, g_ref, u_ref, sel_ref, w_ref, gbuf_ref, dbuf_ref):
    tt, ds, _ = h_ref.shape
    n_sel = g_ref.shape[1]
    words = ds // 2
    mask = _diag_mask(ds, n_sel)

    def token(t, carry):
        _gather_rows(eidx_ref, t, u_ref, gbuf_ref, n_sel, words)
        g2 = pltpu.bitcast(gbuf_ref[...], BF16)
        hs = h_ref[t].astype(BF16)
        r = lax.dot_general(hs, g2, _NT, preferred_element_type=F32)
        dbuf_ref[pl.ds(t, 1), :] = jnp.sum(jnp.where(mask, r, 0.0), axis=0, keepdims=True)
        return carry

    lax.fori_loop(0, tt, token, 0)
    d = dbuf_ref[...]
    d_hi = d.astype(BF16)
    d_lo = (d - d_hi.astype(F32)).astype(BF16)
    sel = sel_ref[...]
    a = (jnp.dot(d_hi, sel, preferred_element_type=F32)
         + jnp.dot(d_lo, sel, preferred_element_type=F32))
    w_ref[...] = g_ref[...] * _gelu(a)


def _peer_v_kernel(eidx_ref, w_ref, x1_ref, v_ref, selt_ref, o_ref, gbuf_ref, wrep_ref):
    tt, ds, _ = x1_ref.shape
    n_sel = w_ref.shape[1]
    words = ds // 2
    mask = _diag_mask(ds, n_sel)
    w = w_ref[...]
    w_hi = w.astype(BF16)
    w_lo = (w - w_hi.astype(F32)).astype(BF16)
    selt = selt_ref[...]
    wrep_ref[...] = (jnp.dot(w_hi, selt, preferred_element_type=F32)
                     + jnp.dot(w_lo, selt, preferred_element_type=F32))

    def token(t, carry):
        _gather_rows(eidx_ref, t, v_ref, gbuf_ref, n_sel, words)
        g2 = pltpu.bitcast(gbuf_ref[...], BF16)
        wr = jnp.broadcast_to(wrep_ref[pl.ds(t, 1), :], mask.shape)
        wm = jnp.where(mask, wr, 0.0)
        wm_hi = wm.astype(BF16)
        wm_lo = (wm - wm_hi.astype(F32)).astype(BF16)
        o = (jnp.dot(wm_hi, g2, preferred_element_type=F32)
             + jnp.dot(wm_lo, g2, preferred_element_type=F32))
        o_ref[t] = x1_ref[t] + o
        return carry

    lax.fori_loop(0, tt, token, 0)


def _sel_matrix(ds, n_sel):
    grp = np.arange(ds * n_sel) // ds
    return (grp[:, None] == np.arange(n_sel)[None, :]).astype(np.float32)


def _peer_u(eidx, h3, g, u_packed, tt):
    n, ds, _ = h3.shape
    n_sel = g.shape[1]
    words = ds // 2
    sel = jnp.asarray(_sel_matrix(ds, n_sel), BF16)
    return pl.pallas_call(
        _peer_u_kernel,
        grid=(n // tt,),
        in_specs=[pl.BlockSpec((tt, n_sel), lambda i: (i, 0), memory_space=pltpu.SMEM),
                  pl.BlockSpec((tt, ds, LANES), lambda i: (i, 0, 0)),
                  pl.BlockSpec((tt, n_sel), lambda i: (i, 0)),
                  _const_spec(u_packed.shape),
                  _const_spec(sel.shape)],
        out_specs=pl.BlockSpec((tt, n_sel), lambda i: (i, 0)),
        out_shape=jax.ShapeDtypeStruct((n, n_sel), F32),
        scratch_shapes=[pltpu.VMEM((n_sel * words, LANES), I32),
                        pltpu.VMEM((tt, n_sel * ds), F32)],
        compiler_params=_cparams(1, VMEM_LIMIT),
        name="peer_u",
    )(eidx, h3, g, u_packed, sel)


def _peer_v(eidx, w, x1_3, v_packed, tt):
    n, ds, _ = x1_3.shape
    n_sel = w.shape[1]
    words = ds // 2
    selt = jnp.asarray(_sel_matrix(ds, n_sel).T, BF16)
    return pl.pallas_call(
        _peer_v_kernel,
        grid=(n // tt,),
        in_specs=[pl.BlockSpec((tt, n_sel), lambda i: (i, 0), memory_space=pltpu.SMEM),
                  pl.BlockSpec((tt, n_sel), lambda i: (i, 0)),
                  pl.BlockSpec((tt, ds, LANES), lambda i: (i, 0, 0)),
                  _const_spec(v_packed.shape),
                  _const_spec(selt.shape)],
        out_specs=pl.BlockSpec((tt, ds, LANES), lambda i: (i, 0, 0)),
        out_shape=jax.ShapeDtypeStruct((n, ds, LANES), F32),
        scratch_shapes=[pltpu.VMEM((n_sel * words, LANES), I32),
                        pltpu.VMEM((tt, n_sel * ds), F32)],
        compiler_params=_cparams(1, VMEM_LIMIT),
        name="peer_v",
    )(eidx, w, x1_3, v_packed, selt)


def _tile(n, pref):
    t = min(n, pref)
    while n % t:
        t //= 2
    return t


def _layer(x, norm_mix_g, w_in, q_norm_g, k_norm_g, lq1, lk1, lq2, lk2, subln_g, w_attn_proj,
           conv_w, w_conv_proj, w_out, norm_ffn_g, w_query, sub_keys, peer_u, peer_v, lam_init):
    b, s, d = x.shape
    n = b * s
    n_heads = d // V_HEAD_DIM
    qk_w = n_heads * 2 * HEAD_DIM
    x2d = x.reshape(n, d)

    wb = w_in.astype(BF16)
    o = 0
    w_qk = jnp.stack([wb[:, 0:qk_w], wb[:, qk_w:2 * qk_w]])
    o = 2 * qk_w
    w_vt = wb[:, o:o + d].T
    o += d
    w_c = wb[:, o:o + 3 * d]
    o += 3 * d
    w_g = jnp.stack([wb[:, o:o + d], wb[:, o + d:o + 2 * d]])
    reps = qk_w // HEAD_DIM
    gains = jnp.stack([jnp.tile(q_norm_g.astype(F32), reps) * (HEAD_DIM ** -0.5),
                       jnp.tile(k_norm_g.astype(F32), reps)]).reshape(2, 1, qk_w)
    slopes = jnp.asarray(np.array([2.0 ** (-8.0 * (i + 1) / n_heads) for i in range(n_heads)],
                                  dtype=np.float32))

    h = _rmsnorm(x2d, norm_mix_g.astype(F32), _tile(n, 1024))
    h3d = h.reshape(b, s, d)
    qk = _qk_proj(h, w_qk, gains, _tile(n, 1024))
    vt = _vt_proj(h3d, w_vt, _tile(s, 512))
    sg = _gate_proj(h, w_g, _tile(n, 1024))
    convb = _conv_branch(h3d, w_c, conv_w.astype(F32), _tile(s, 512))
    attn = _diff_attention(qk[0].reshape(b, s, qk_w), qk[1].reshape(b, s, qk_w), vt, slopes,
                           lq1.astype(F32), lk1.astype(F32), lq2.astype(F32), lk2.astype(F32),
                           subln_g.astype(F32), lam_init, _tile(s, 256))
    x1, h2, qp = _mix(x2d, attn.reshape(n, d), convb.reshape(n, d), sg,
                      w_attn_proj.astype(BF16), w_conv_proj.astype(BF16), w_out.astype(BF16),
                      norm_ffn_g.astype(F32), w_query.astype(BF16), _tile(n, 512))

    skb = sub_keys.astype(BF16).reshape(PEER_HEADS * 2, N_KEYS, D_HALF)
    eidx, g = _peer_topk(qp, skb, _tile(n, 256))
    ds = d // LANES
    tt = _tile(n, 128)
    w = _peer_u(eidx, h2.reshape(n, ds, LANES), g, _pack_table(peer_u), tt)
    out = _peer_v(eidx, w, x1.reshape(n, ds, LANES), _pack_table(peer_v), tt)
    return out.reshape(b, s, d)


def kernel(x, norm_mix_g, w_in, q_norm_g, k_norm_g, lambda_q1, lambda_k1, lambda_q2, lambda_k2, subln_g, w_attn_proj, conv_w, w_conv_proj, w_out, norm_ffn_g, peer_w_query, peer_sub_keys, peer_u, peer_v):
    depth = w_in.shape[0]
    for l in range(depth):
        lam_init = 0.8 - 0.6 * math.exp(-0.3 * l)
        x = _layer(x, norm_mix_g[l], w_in[l], q_norm_g[l], k_norm_g[l], lambda_q1[l], lambda_k1[l],
                   lambda_q2[l], lambda_k2[l], subln_g[l], w_attn_proj[l], conv_w[l], w_conv_proj[l],
                   w_out[l], norm_ffn_g[l], peer_w_query[l], peer_sub_keys[l], peer_u[l], peer_v[l],
                   lam_init)
    return x
```

```python
import functools
import math

import numpy as np
import jax
import jax.numpy as jnp
from jax import lax
from jax.experimental import pallas as pl
from jax.experimental.pallas import tpu as pltpu

F32 = jnp.float32
BF16 = jnp.bfloat16
I32 = jnp.int32

EPS = 1e-6
HEAD_DIM = 64
V_HEAD_DIM = 2 * HEAD_DIM
CONV_K = 3
PEER_HEADS = 8
N_KEYS = 128
PEER_TOPK = 16
D_HALF = 128
LANES = 128
SUBLANES = 8
MXU_COL = 256
NEG = -1e30
VMEM_LIMIT = 56 * 1024 * 1024

_NT = (((1,), (1,)), ((), ()))


def _cparams(n_axes, vmem=None):
    return pltpu.CompilerParams(
        dimension_semantics=("arbitrary",) * n_axes,
        vmem_limit_bytes=vmem)


def _const_spec(shape):
    nd = len(shape)
    return pl.BlockSpec(shape, lambda *_: (0,) * nd, pipeline_mode=pl.Buffered(1))


def _rmsnorm_kernel(x_ref, g_ref, o_ref):
    x = x_ref[...]
    ms = jnp.mean(x * x, axis=-1, keepdims=True)
    o_ref[...] = ((x * lax.rsqrt(ms + EPS)) * g_ref[...]).astype(o_ref.dtype)


def _rmsnorm(x2d, g, tm):
    n, d = x2d.shape
    return pl.pallas_call(
        _rmsnorm_kernel,
        grid=(n // tm,),
        in_specs=[pl.BlockSpec((tm, d), lambda i: (i, 0)), _const_spec((1, d))],
        out_specs=pl.BlockSpec((tm, d), lambda i: (i, 0)),
        out_shape=jax.ShapeDtypeStruct((n, d), BF16),
        compiler_params=_cparams(1),
        name="rmsnorm_in",
    )(x2d, g.reshape(1, d))


def _qk_kernel(h_ref, w_ref, g_ref, bd_ref, o_ref):
    p = jnp.dot(h_ref[...], w_ref[...], preferred_element_type=F32)
    sq = p * p
    hi = sq.astype(BF16)
    lo = (sq - hi.astype(F32)).astype(BF16)
    bd = bd_ref[...]
    parts = []
    for c in range(p.shape[1] // MXU_COL):
        sl = slice(c * MXU_COL, (c + 1) * MXU_COL)
        parts.append(jnp.dot(hi[:, sl], bd, preferred_element_type=F32)
                     + jnp.dot(lo[:, sl], bd, preferred_element_type=F32))
    ss = jnp.concatenate(parts, axis=1)
    y = p * lax.rsqrt(ss * (1.0 / HEAD_DIM) + EPS)
    o_ref[...] = (y * g_ref[...]).astype(o_ref.dtype)


def _qk_proj(h2d, w_qk, gains, tm):
    n, d = h2d.shape
    width = w_qk.shape[2]
    grp = np.arange(MXU_COL) // HEAD_DIM
    bd = jnp.asarray((grp[:, None] == grp[None, :]).astype(np.float32), BF16)
    return pl.pallas_call(
        _qk_kernel,
        grid=(2, n // tm),
        in_specs=[
            pl.BlockSpec((tm, d), lambda g, i: (i, 0)),
            pl.BlockSpec((None, d, width), lambda g, i: (g, 0, 0)),
            pl.BlockSpec((None, 1, width), lambda g, i: (g, 0, 0)),
            _const_spec((MXU_COL, MXU_COL)),
        ],
        out_specs=pl.BlockSpec((None, tm, width), lambda g, i: (g, i, 0)),
        out_shape=jax.ShapeDtypeStruct((2, n, width), BF16),
        compiler_params=_cparams(2, VMEM_LIMIT),
        name="qk_proj",
    )(h2d, w_qk, gains, bd)


def _vt_kernel(h_ref, wt_ref, o_ref):
    o_ref[...] = lax.dot_general(wt_ref[...], h_ref[...], _NT,
                                 preferred_element_type=F32).astype(o_ref.dtype)


def _vt_proj(h3d, w_vt, tm):
    b, s, d = h3d.shape
    dout = w_vt.shape[0]
    return pl.pallas_call(
        _vt_kernel,
        grid=(b, s // tm),
        in_specs=[pl.BlockSpec((None, tm, d), lambda i, j: (i, j, 0)),
                  _const_spec((dout, d))],
        out_specs=pl.BlockSpec((None, dout, tm), lambda i, j: (i, 0, j)),
        out_shape=jax.ShapeDtypeStruct((b, dout, s), BF16),
        compiler_params=_cparams(2, VMEM_LIMIT),
        name="vt_proj",
    )(h3d, w_vt)


def _gate_kernel(h_ref, w_ref, o_ref):
    p = jnp.dot(h_ref[...], w_ref[...], preferred_element_type=F32)
    o_ref[...] = (1.0 / (1.0 + jnp.exp(-p))).astype(o_ref.dtype)


def _gate_proj(h2d, w_g, tm):
    n, d = h2d.shape
    width = w_g.shape[2]
    return pl.pallas_call(
        _gate_kernel,
        grid=(2, n // tm),
        in_specs=[pl.BlockSpec((tm, d), lambda g, i: (i, 0)),
                  pl.BlockSpec((None, d, width), lambda g, i: (g, 0, 0))],
        out_specs=pl.BlockSpec((None, tm, width), lambda g, i: (g, i, 0)),
        out_shape=jax.ShapeDtypeStruct((2, n, width), BF16),
        compiler_params=_cparams(2, VMEM_LIMIT),
        name="gate_proj",
    )(h2d, w_g)


def _conv_kernel(h_ref, w_ref, cw_ref, o_ref, carry_ref):
    @pl.when(pl.program_id(1) == 0)
    def _():
        carry_ref[...] = jnp.zeros_like(carry_ref)

    width = o_ref.shape[1]
    tm = o_ref.shape[0]
    p = jnp.dot(h_ref[...], w_ref[...], preferred_element_type=F32)
    y = p[:, width:2 * width] * p[:, 2 * width:]
    rows = lax.broadcasted_iota(I32, y.shape, 0)
    prev2 = carry_ref[SUBLANES - 2:SUBLANES - 1, :]
    prev1 = carry_ref[SUBLANES - 1:SUBLANES, :]
    y1 = jnp.where(rows == 0, prev1, pltpu.roll(y, 1, 0))
    y2 = jnp.where(rows == 0, prev2, jnp.where(rows == 1, prev1, pltpu.roll(y, 2, 0)))
    cw = cw_ref[...]
    z = cw[0:1, :] * y2 + cw[1:2, :] * y1 + cw[2:3, :] * y
    o_ref[...] = (p[:, :width] * z).astype(o_ref.dtype)
    carry_ref[...] = y[tm - SUBLANES:, :]


def _conv_branch(h3d, w_c, conv_w, tm):
    b, s, d = h3d.shape
    width = conv_w.shape[1]
    return pl.pallas_call(
        _conv_kernel,
        grid=(b, s // tm),
        in_specs=[pl.BlockSpec((None, tm, d), lambda i, j: (i, j, 0)),
                  _const_spec((d, 3 * width)),
                  _const_spec((CONV_K, width))],
        out_specs=pl.BlockSpec((None, tm, width), lambda i, j: (i, j, 0)),
        out_shape=jax.ShapeDtypeStruct((b, s, width), BF16),
        scratch_shapes=[pltpu.VMEM((SUBLANES, width), F32)],
        compiler_params=_cparams(2, VMEM_LIMIT),
        name="conv_branch",
    )(h3d, w_c, conv_w)


def _attn_kernel(slopes_ref, q_ref, k_ref, vt_ref, lq1_ref, lk1_ref, lq2_ref, lk2_ref, sg_ref,
                 o_ref, kaug_ref, qaug_ref, m_ref, l_ref, acc_ref, s0_ref, *, tq, cw, lam_init):
    h = pl.program_id(1)
    qi = pl.program_id(2)
    s_len = k_ref.shape[0]
    slope = slopes_ref[h]

    @pl.when(qi == 0)
    def _():
        kpos = lax.broadcasted_iota(I32, (s_len, LANES), 0).astype(F32) * slope
        lane = lax.broadcasted_iota(I32, (s_len, LANES), 1)
        b_hi = kpos.astype(BF16).astype(F32)
        b_lo = kpos - b_hi
        extra = jnp.where(lane == 0, b_hi, jnp.where(lane == 1, b_lo, jnp.where(lane < 4, 1.0, 0.0)))
        kaug_ref[:, :LANES] = k_ref[...]
        kaug_ref[:, LANES:] = extra.astype(BF16)

    q = q_ref[...].astype(F32)
    lane = lax.broadcasted_iota(I32, (tq, LANES), 1)
    c_full = -((jnp.zeros((tq, LANES), I32) + qi * tq).astype(F32) * slope)
    c_hi = c_full.astype(BF16).astype(F32)
    c_lo = c_full - c_hi
    qextra = jnp.where(lane < 2, 1.0, jnp.where(lane == 2, c_hi, jnp.where(lane == 3, c_lo, 0.0)))
    qextra = qextra.astype(BF16)
    qaug_ref[:tq, :LANES] = jnp.where(lane < HEAD_DIM, q, 0.0).astype(BF16)
    qaug_ref[tq:, :LANES] = jnp.where(lane >= HEAD_DIM, q, 0.0).astype(BF16)
    qaug_ref[:tq, LANES:] = qextra
    qaug_ref[tq:, LANES:] = qextra

    m_ref[...] = jnp.full(m_ref.shape, NEG, F32)
    l_ref[...] = jnp.zeros(l_ref.shape, F32)
    acc_ref[...] = jnp.zeros(acc_ref.shape, F32)

    def scores(kt, c, masked):
        lo = c * cw
        q_lo = lo % tq
        kn = q_lo + cw if masked else tq
        k0 = pl.multiple_of(kt * tq, tq)
        s = lax.dot_general(kaug_ref[pl.ds(k0, kn), :], qaug_ref[lo:lo + cw, :], _NT,
                            preferred_element_type=F32)
        if masked:
            kk = lax.broadcasted_iota(I32, s.shape, 0)
            qq = lax.broadcasted_iota(I32, s.shape, 1) + q_lo
            s = jnp.where(kk <= qq, s, NEG)
        return s

    def update(kt, c, s):
        lo = c * cw
        kn = s.shape[0]
        k0 = pl.multiple_of(kt * tq, tq)
        m_old = m_ref[:, lo:lo + cw]
        m_new = jnp.maximum(m_old, jnp.max(s, axis=0, keepdims=True))
        alpha = jnp.exp(m_old - m_new)
        p = jnp.exp(s - m_new)
        l_ref[:, lo:lo + cw] = alpha * l_ref[:, lo:lo + cw] + jnp.sum(p, axis=0, keepdims=True)
        pv = jnp.dot(vt_ref[:, pl.ds(k0, kn)], p.astype(BF16), preferred_element_type=F32)
        acc_ref[:, lo:lo + cw] = alpha * acc_ref[:, lo:lo + cw] + pv
        m_ref[:, lo:lo + cw] = m_new

    def step(kt, masked):
        n_chunks = 2 * tq // cw
        if masked:
            s_next = s0_ref[:cw, :]
            kk = lax.broadcasted_iota(I32, s_next.shape, 0)
            qq = lax.broadcasted_iota(I32, s_next.shape, 1)
            s_next = jnp.where(kk <= qq, s_next, NEG)
        else:
            s_next = s0_ref[...]
        for c in range(n_chunks):
            s_cur = s_next
            if c + 1 < n_chunks:
                s_next = scores(kt, c + 1, masked)
            update(kt, c, s_cur)
        if not masked:
            s0_ref[...] = scores(kt + 1, 0, False)

    def body(kt, carry):
        step(kt, False)
        return carry

    s0_ref[...] = scores(0, 0, False)
    lax.fori_loop(0, qi, body, 0)
    step(qi, True)

    lam = (jnp.exp(jnp.sum(lq1_ref[...] * lk1_ref[...], axis=1, keepdims=True))
           - jnp.exp(jnp.sum(lq2_ref[...] * lk2_ref[...], axis=1, keepdims=True))
           + lam_init)
    o = acc_ref[...] / l_ref[...]
    a = o[:, :tq] - lam * o[:, tq:]
    ms = jnp.mean(a * a, axis=0, keepdims=True)
    a = (a * lax.rsqrt(ms + EPS)) * sg_ref[...]
    a = a * (1.0 - lam_init)
    o_ref[...] = a.T.astype(o_ref.dtype)


def _diff_attention(q3d, k3d, vt3d, slopes, lq1, lk1, lq2, lk2, subln_g, lam_init, tq, cw):
    b, s, width = q3d.shape
    n_heads = width // V_HEAD_DIM
    lam_spec = pl.BlockSpec((1, HEAD_DIM), lambda i, h, j, *_: (0, 0))
    grid_spec = pltpu.PrefetchScalarGridSpec(
        num_scalar_prefetch=1,
        grid=(b, n_heads, s // tq),
        in_specs=[
            pl.BlockSpec((None, tq, V_HEAD_DIM), lambda i, h, j, *_: (i, j, h)),
            pl.BlockSpec((None, s, V_HEAD_DIM), lambda i, h, j, *_: (i, 0, h)),
            pl.BlockSpec((None, V_HEAD_DIM, s), lambda i, h, j, *_: (i, h, 0)),
            lam_spec, lam_spec, lam_spec, lam_spec,
            pl.BlockSpec((V_HEAD_DIM, 1), lambda i, h, j, *_: (0, 0)),
        ],
        out_specs=pl.BlockSpec((None, tq, V_HEAD_DIM), lambda i, h, j, *_: (i, j, h)),
        scratch_shapes=[
            pltpu.VMEM((s, 2 * LANES), BF16),
            pltpu.VMEM((2 * tq, 2 * LANES), BF16),
            pltpu.VMEM((1, 2 * tq), F32),
            pltpu.VMEM((1, 2 * tq), F32),
            pltpu.VMEM((V_HEAD_DIM, 2 * tq), F32),
            pltpu.VMEM((tq, cw), F32),
        ],
    )
    return pl.pallas_call(
        functools.partial(_attn_kernel, tq=tq, cw=cw, lam_init=lam_init),
        grid_spec=grid_spec,
        out_shape=jax.ShapeDtypeStruct((b, s, width), BF16),
        compiler_params=_cparams(3, VMEM_LIMIT),
        name="diff_attention",
    )(slopes, q3d, k3d, vt3d, lq1.reshape(1, -1), lk1.reshape(1, -1), lq2.reshape(1, -1),
      lk2.reshape(1, -1), subln_g.reshape(-1, 1))


def _mix_kernel(x_ref, attn_ref, conv_ref, sga_ref, sgc_ref, wa_ref, wc_ref, wo_ref, gf_ref, wq_ref,
                x1_ref, h2_ref, qp_ref):
    a = jnp.dot(attn_ref[...], wa_ref[...], preferred_element_type=F32)
    c = jnp.dot(conv_ref[...], wc_ref[...], preferred_element_type=F32)
    mixed = sga_ref[...].astype(F32) * a + sgc_ref[...].astype(F32) * c
    x1 = x_ref[...] + jnp.dot(mixed.astype(BF16), wo_ref[...], preferred_element_type=F32)
    ms = jnp.mean(x1 * x1, axis=-1, keepdims=True)
    h2 = (x1 * lax.rsqrt(ms + EPS)) * gf_ref[...]
    for s in range(x1_ref.shape[1]):
        x1_ref[:, s, :] = x1[:, s * LANES:(s + 1) * LANES]
        h2_ref[:, s, :] = h2[:, s * LANES:(s + 1) * LANES]
    qp_ref[...] = jnp.dot(h2.astype(BF16), wq_ref[...], preferred_element_type=F32).astype(qp_ref.dtype)


def _mix(x2d, attn2d, conv2d, sg, wa, wc, wo, gf, wq, tm):
    n, d = x2d.shape
    qw = wq.shape[1]
    row = lambda width: pl.BlockSpec((tm, width), lambda i: (i, 0))
    tiled = pl.BlockSpec((tm, d // LANES, LANES), lambda i: (i, 0, 0))
    return pl.pallas_call(
        _mix_kernel,
        grid=(n // tm,),
        in_specs=[row(d), row(d), row(d),
                  pl.BlockSpec((None, tm, d), lambda i: (0, i, 0)),
                  pl.BlockSpec((None, tm, d), lambda i: (1, i, 0)),
                  _const_spec((d, d)), _const_spec((d, d)), _const_spec((d, d)),
                  _const_spec((1, d)), _const_spec((d, qw))],
        out_specs=[tiled, tiled, row(qw)],
        out_shape=[jax.ShapeDtypeStruct((n, d // LANES, LANES), F32),
                   jax.ShapeDtypeStruct((n, d // LANES, LANES), F32),
                   jax.ShapeDtypeStruct((n, qw), BF16)],
        compiler_params=_cparams(1, VMEM_LIMIT),
        name="mix_residual",
    )(x2d, attn2d, conv2d, sg, sg, wa, wc, wo, gf.reshape(1, d), wq)


def _top_rounds(c, rowid, payload, n_rounds):
    vals, rids, pays = [], [], []
    for _ in range(n_rounds):
        m = jnp.max(c, axis=0, keepdims=True)
        rid = jnp.min(jnp.where(c == m, rowid, 1e9), axis=0, keepdims=True)
        hit = rowid == rid
        vals.append(m)
        rids.append(rid)
        if payload is not None:
            pays.append(jnp.sum(jnp.where(hit, payload, 0.0), axis=0, keepdims=True))
        c = jnp.where(hit, NEG, c)
    return vals, rids, pays


def _stack_rows(rows):
    n, tt = len(rows), rows[0].shape[1]
    rid = lax.broadcasted_iota(I32, (n, tt), 0)
    out = jnp.zeros((n, tt), rows[0].dtype)
    for r, row in enumerate(rows):
        out = jnp.where(rid == r, row, out)
    return out


def _candidates(sv, si):
    assert PEER_TOPK == 2 * SUBLANES
    tt = sv[0].shape[1]
    sub = lax.broadcasted_iota(I32, (SUBLANES, tt), 0)
    subf = sub.astype(F32)
    vals, ids, exps = [], [], []
    for b0 in (0, SUBLANES):
        vals.append(sv[0][0:1, :] + sv[1][b0:b0 + SUBLANES, :])
        ids.append(subf + float(b0))
        exps.append(si[0][0:1, :] * float(N_KEYS) + si[1][b0:b0 + SUBLANES, :])
    for a in range(1, SUBLANES):
        v = sv[0][a:a + 1, :] + sv[1][0:SUBLANES, :]
        vals.append(jnp.where(sub < PEER_TOPK // (a + 1), v, NEG))
        ids.append(subf + float(a * PEER_TOPK))
        exps.append(si[0][a:a + 1, :] * float(N_KEYS) + si[1][0:SUBLANES, :])
    vals.append(sv[0][SUBLANES:, :] + sv[1][0:1, :])
    ids.append((subf + float(SUBLANES)) * float(PEER_TOPK))
    exps.append(si[0][SUBLANES:, :] * float(N_KEYS) + si[1][0:1, :])
    cat = lambda xs: jnp.concatenate(xs, axis=0)
    return cat(vals), cat(ids), cat(exps)


def _topk_kernel(qp_ref, sk_ref, eidx_ref, g_ref, *, row_words):
    tt = qp_ref.shape[0]
    key_id = lax.broadcasted_iota(I32, (N_KEYS, tt), 0).astype(F32)
    e_rows, g_rows = [], []
    for h in range(PEER_HEADS):
        sv, si = [], []
        for p in range(2):
            hp = 2 * h + p
            q = qp_ref[:, hp * D_HALF:(hp + 1) * D_HALF]
            s = lax.dot_general(sk_ref[hp], q, _NT, preferred_element_type=F32)
            vals, rids, _ = _top_rounds(s, key_id, None, PEER_TOPK)
            sv.append(_stack_rows(vals))
            si.append(_stack_rows(rids))
        cand, cand_id, cidx = _candidates(sv, si)
        tv, _, te = _top_rounds(cand, cand_id, cidx, PEER_TOPK)
        tv = _stack_rows(tv)
        ex = jnp.exp(tv - tv[0:1, :])
        g_rows.append(ex / jnp.sum(ex, axis=0, keepdims=True))
        e_rows.append(_stack_rows(te))
    eidx_ref[...] = (jnp.concatenate(e_rows, axis=0).T * float(row_words)).astype(I32)
    g_ref[...] = jnp.concatenate(g_rows, axis=0).T


def _peer_topk(qp, sub_keys, row_words, tt):
    n, qw = qp.shape
    nk = PEER_HEADS * PEER_TOPK
    return pl.pallas_call(
        functools.partial(_topk_kernel, row_words=row_words),
        grid=(n // tt,),
        in_specs=[pl.BlockSpec((tt, qw), lambda i: (i, 0)),
                  _const_spec(sub_keys.shape)],
        out_specs=[pl.BlockSpec((tt, nk), lambda i: (i, 0)),
                   pl.BlockSpec((tt, nk), lambda i: (i, 0))],
        out_shape=[jax.ShapeDtypeStruct((n, nk), I32),
                   jax.ShapeDtypeStruct((n, nk), F32)],
        compiler_params=_cparams(1, VMEM_LIMIT),
        name="peer_topk",
    )(qp, sub_keys)


def _pack_table(t):
    e, d = t.shape
    words = d // (2 * LANES)
    tb = t.astype(BF16).reshape(e, words, 2, LANES)
    tb = jnp.swapaxes(tb, 2, 3)
    return lax.bitcast_convert_type(tb, I32).reshape(e * words, LANES)


def _gather_rows(idx_row, tab_ref, gbuf_ref, n_sel, words):
    for k in range(n_sel):
        r = pl.multiple_of(idx_row[k], words)
        gbuf_ref[pl.ds(k * words, words), :] = tab_ref[pl.ds(r, words), :]


def _diag_mask(rows, ds, n_sel):
    col = lax.broadcasted_iota(I32, (rows, ds * n_sel), 1)
    row = lax.broadcasted_iota(I32, (rows, ds * n_sel), 0)
    return (col % ds) == (row % ds)


def _gelu(a):
    return 0.5 * a * (1.0 + lax.erf(a * (1.0 / math.sqrt(2.0))))


def _split_bf16(x):
    hi = x.astype(BF16)
    return hi, (x - hi.astype(F32)).astype(BF16)


def _peer_u_kernel(eidx_ref, h_ref, g_ref, u_ref, sel_ref, fold_ref, w_ref, gbuf_ref, dbuf_ref, *, unroll):
    tt, ds, _ = h_ref.shape
    n_sel = g_ref.shape[1]
    words = ds // 2
    mask = _diag_mask(ds, ds, n_sel)

    def group(i, carry):
        for j in range(unroll):
            _gather_rows(eidx_ref.at[i * unroll + j], u_ref, gbuf_ref.at[j], n_sel, words)
        for j in range(unroll):
            t = i * unroll + j
            g2 = pltpu.bitcast(gbuf_ref[j], BF16)
            hs = h_ref[t].astype(BF16)
            r = lax.dot_general(hs, g2, _NT, preferred_element_type=F32)
            dbuf_ref[pl.ds(pl.multiple_of(t * ds, ds), ds), :] = jnp.where(mask, r, 0.0)
        return carry

    lax.fori_loop(0, tt // unroll, group, 0)
    sel = sel_ref[...]
    d_hi, d_lo = _split_bf16(dbuf_ref[...])
    a8 = (jnp.dot(d_hi, sel, preferred_element_type=F32)
          + jnp.dot(d_lo, sel, preferred_element_type=F32))
    a_hi, a_lo = _split_bf16(a8)
    fold = fold_ref[...]
    a = (jnp.dot(fold, a_hi, preferred_element_type=F32)
         + jnp.dot(fold, a_lo, preferred_element_type=F32))
    w_ref[...] = g_ref[...] * _gelu(a)


def _peer_v_kernel(eidx_ref, w_ref, x1_ref, v_ref, selt_ref, rep_ref, o_ref, gbuf_ref, whi_ref, wlo_ref,
                   *, unroll):
    tt, ds, _ = x1_ref.shape
    n_sel = w_ref.shape[1]
    words = ds // 2
    mask = _diag_mask(tt * ds, ds, n_sel)
    selt = selt_ref[...]
    rep = rep_ref[...]
    for part, dst in zip(_split_bf16(w_ref[...]), (whi_ref, wlo_ref)):
        rows = jnp.dot(rep, part, preferred_element_type=F32).astype(BF16)
        dst[...] = jnp.where(mask, jnp.dot(rows, selt, preferred_element_type=F32), 0.0)

    def group(i, carry):
        for j in range(unroll):
            _gather_rows(eidx_ref.at[i * unroll + j], v_ref, gbuf_ref.at[j], n_sel, words)
        for j in range(unroll):
            t = i * unroll + j
            g2 = pltpu.bitcast(gbuf_ref[j], BF16)
            r0 = pl.multiple_of(t * ds, ds)
            o = (jnp.dot(whi_ref[pl.ds(r0, ds), :].astype(BF16), g2, preferred_element_type=F32)
                 + jnp.dot(wlo_ref[pl.ds(r0, ds), :].astype(BF16), g2, preferred_element_type=F32))
            o_ref[t] = x1_ref[t] + o
        return carry

    lax.fori_loop(0, tt // unroll, group, 0)


def _sel_matrix(ds, n_sel):
    grp = np.arange(ds * n_sel) // ds
    return (grp[:, None] == np.arange(n_sel)[None, :]).astype(np.float32)


def _fold_matrix(tt, ds):
    grp = np.arange(tt * ds) // ds
    return (np.arange(tt)[:, None] == grp[None, :]).astype(np.float32)


PEER_UNROLL = 4


def _peer_u(eidx, h3, g, u_packed, tt):
    n, ds, _ = h3.shape
    n_sel = g.shape[1]
    words = ds // 2
    sel = jnp.asarray(_sel_matrix(ds, n_sel), BF16)
    fold = jnp.asarray(_fold_matrix(tt, ds), BF16)
    return pl.pallas_call(
        functools.partial(_peer_u_kernel, unroll=PEER_UNROLL),
        grid=(n // tt,),
        in_specs=[pl.BlockSpec((tt, n_sel), lambda i: (i, 0), memory_space=pltpu.SMEM),
                  pl.BlockSpec((tt, ds, LANES), lambda i: (i, 0, 0)),
                  pl.BlockSpec((tt, n_sel), lambda i: (i, 0)),
                  _const_spec(u_packed.shape),
                  _const_spec(sel.shape),
                  _const_spec(fold.shape)],
        out_specs=pl.BlockSpec((tt, n_sel), lambda i: (i, 0)),
        out_shape=jax.ShapeDtypeStruct((n, n_sel), F32),
        scratch_shapes=[pltpu.VMEM((PEER_UNROLL, n_sel * words, LANES), I32),
                        pltpu.VMEM((tt * ds, n_sel * ds), F32)],
        compiler_params=_cparams(1, VMEM_LIMIT),
        name="peer_u",
    )(eidx, h3, g, u_packed, sel, fold)


def _peer_v(eidx, w, x1_3, v_packed, tt):
    n, ds, _ = x1_3.shape
    n_sel = w.shape[1]
    words = ds // 2
    selt = jnp.asarray(_sel_matrix(ds, n_sel).T, BF16)
    rep = jnp.asarray(_fold_matrix(tt, ds).T, BF16)
    return pl.pallas_call(
        functools.partial(_peer_v_kernel, unroll=PEER_UNROLL),
        grid=(n // tt,),
        in_specs=[pl.BlockSpec((tt, n_sel), lambda i: (i, 0), memory_space=pltpu.SMEM),
                  pl.BlockSpec((tt, n_sel), lambda i: (i, 0)),
                  pl.BlockSpec((tt, ds, LANES), lambda i: (i, 0, 0)),
                  _const_spec(v_packed.shape),
                  _const_spec(selt.shape),
                  _const_spec(rep.shape)],
        out_specs=pl.BlockSpec((tt, ds, LANES), lambda i: (i, 0, 0)),
        out_shape=jax.ShapeDtypeStruct((n, ds, LANES), F32),
        scratch_shapes=[pltpu.VMEM((PEER_UNROLL, n_sel * words, LANES), I32),
                        pltpu.VMEM((tt * ds, n_sel * ds), F32),
                        pltpu.VMEM((tt * ds, n_sel * ds), F32)],
        compiler_params=_cparams(1, VMEM_LIMIT),
        name="peer_v",
    )(eidx, w, x1_3, v_packed, selt, rep)


def _tile(n, pref):
    t = min(n, pref)
    while n % t:
        t //= 2
    return t


def _layer(x, norm_mix_g, w_in, q_norm_g, k_norm_g, lq1, lk1, lq2, lk2, subln_g, w_attn_proj,
           conv_w, w_conv_proj, w_out, norm_ffn_g, w_query, sub_keys, peer_u, peer_v, lam_init):
    b, s, d = x.shape
    n = b * s
    n_heads = d // V_HEAD_DIM
    qk_w = n_heads * 2 * HEAD_DIM
    x2d = x.reshape(n, d)

    wb = w_in.astype(BF16)
    o = 0
    w_qk = jnp.stack([wb[:, 0:qk_w], wb[:, qk_w:2 * qk_w]])
    o = 2 * qk_w
    w_vt = wb[:, o:o + d].T
    o += d
    w_c = wb[:, o:o + 3 * d]
    o += 3 * d
    w_g = jnp.stack([wb[:, o:o + d], wb[:, o + d:o + 2 * d]])
    reps = qk_w // HEAD_DIM
    gains = jnp.stack([jnp.tile(q_norm_g.astype(F32), reps) * (HEAD_DIM ** -0.5),
                       jnp.tile(k_norm_g.astype(F32), reps)]).reshape(2, 1, qk_w)
    slopes = jnp.asarray(np.array([2.0 ** (-8.0 * (i + 1) / n_heads) for i in range(n_heads)],
                                  dtype=np.float32))

    h = _rmsnorm(x2d, norm_mix_g.astype(F32), _tile(n, 1024))
    h3d = h.reshape(b, s, d)
    qk = _qk_proj(h, w_qk, gains, _tile(n, 1024))
    vt = _vt_proj(h3d, w_vt, _tile(s, 512))
    sg = _gate_proj(h, w_g, _tile(n, 1024))
    convb = _conv_branch(h3d, w_c, conv_w.astype(F32), _tile(s, 512))
    attn = _diff_attention(qk[0].reshape(b, s, qk_w), qk[1].reshape(b, s, qk_w), vt, slopes,
                           lq1.astype(F32), lk1.astype(F32), lq2.astype(F32), lk2.astype(F32),
                           subln_g.astype(F32), lam_init, _tile(s, 512), MXU_COL)
    x1, h2, qp = _mix(x2d, attn.reshape(n, d), convb.reshape(n, d), sg,
                      w_attn_proj.astype(BF16), w_conv_proj.astype(BF16), w_out.astype(BF16),
                      norm_ffn_g.astype(F32), w_query.astype(BF16), _tile(n, 512))

    skb = sub_keys.astype(BF16).reshape(PEER_HEADS * 2, N_KEYS, D_HALF)
    ds = d // LANES
    eidx, g = _peer_topk(qp, skb, ds // 2, _tile(n, 256))
    tt = _tile(n, 128)
    w = _peer_u(eidx, h2, g, _pack_table(peer_u), tt)
    out = _peer_v(eidx, w, x1, _pack_table(peer_v), tt)
    return out.reshape(b, s, d)


def kernel(x, norm_mix_g, w_in, q_norm_g, k_norm_g, lambda_q1, lambda_k1, lambda_q2, lambda_k2, subln_g, w_attn_proj, conv_w, w_conv_proj, w_out, norm_ffn_g, peer_w_query, peer_sub_keys, peer_u, peer_v):
    depth = w_in.shape[0]
    for l in range(depth):
        lam_init = 0.8 - 0.6 * math.exp(-0.3 * l)
        x = _layer(x, norm_mix_g[l], w_in[l], q_norm_g[l], k_norm_g[l], lambda_q1[l], lambda_k1[l],
                   lambda_q2[l], lambda_k2[l], subln_g[l], w_attn_proj[l], conv_w[l], w_conv_proj[l],
                   w_out[l], norm_ffn_g[l], peer_w_query[l], peer_sub_keys[l], peer_u[l], peer_v[l],
                   lam_init)
    return x
```

```python
import functools
import math

import numpy as np
import jax
import jax.numpy as jnp
from jax import lax
from jax.experimental import pallas as pl
from jax.experimental.pallas import tpu as pltpu

F32 = jnp.float32
BF16 = jnp.bfloat16
I32 = jnp.int32

EPS = 1e-6
HEAD_DIM = 64
V_HEAD_DIM = 2 * HEAD_DIM
CONV_K = 3
PEER_HEADS = 8
N_KEYS = 128
PEER_TOPK = 16
D_HALF = 128
LANES = 128
SUBLANES = 8
MXU_COL = 256
NEG = -1e30
VMEM_LIMIT = 56 * 1024 * 1024

_NT = (((1,), (1,)), ((), ()))


def _cparams(n_axes, vmem=None):
    return pltpu.CompilerParams(
        dimension_semantics=("arbitrary",) * n_axes,
        vmem_limit_bytes=vmem)


def _const_spec(shape):
    nd = len(shape)
    return pl.BlockSpec(shape, lambda *_: (0,) * nd, pipeline_mode=pl.Buffered(1))


def _rmsnorm_kernel(x_ref, g_ref, o_ref):
    x = x_ref[...]
    ms = jnp.mean(x * x, axis=-1, keepdims=True)
    o_ref[...] = ((x * lax.rsqrt(ms + EPS)) * g_ref[...]).astype(o_ref.dtype)


def _rmsnorm(x2d, g, tm):
    n, d = x2d.shape
    return pl.pallas_call(
        _rmsnorm_kernel,
        grid=(n // tm,),
        in_specs=[pl.BlockSpec((tm, d), lambda i: (i, 0)), _const_spec((1, d))],
        out_specs=pl.BlockSpec((tm, d), lambda i: (i, 0)),
        out_shape=jax.ShapeDtypeStruct((n, d), BF16),
        compiler_params=_cparams(1),
        name="rmsnorm_in",
    )(x2d, g.reshape(1, d))


def _qk_kernel(h_ref, w_ref, g_ref, bd_ref, o_ref):
    p = jnp.dot(h_ref[...], w_ref[...], preferred_element_type=F32)
    sq = p * p
    hi = sq.astype(BF16)
    lo = (sq - hi.astype(F32)).astype(BF16)
    bd = bd_ref[...]
    parts = []
    for c in range(p.shape[1] // MXU_COL):
        sl = slice(c * MXU_COL, (c + 1) * MXU_COL)
        parts.append(jnp.dot(hi[:, sl], bd, preferred_element_type=F32)
                     + jnp.dot(lo[:, sl], bd, preferred_element_type=F32))
    ss = jnp.concatenate(parts, axis=1)
    y = p * lax.rsqrt(ss * (1.0 / HEAD_DIM) + EPS)
    o_ref[...] = (y * g_ref[...]).astype(o_ref.dtype)


def _qk_proj(h2d, w_qk, gains, tm):
    n, d = h2d.shape
    width = w_qk.shape[2]
    grp = np.arange(MXU_COL) // HEAD_DIM
    bd = jnp.asarray((grp[:, None] == grp[None, :]).astype(np.float32), BF16)
    return pl.pallas_call(
        _qk_kernel,
        grid=(2, n // tm),
        in_specs=[
            pl.BlockSpec((tm, d), lambda g, i: (i, 0)),
            pl.BlockSpec((None, d, width), lambda g, i: (g, 0, 0)),
            pl.BlockSpec((None, 1, width), lambda g, i: (g, 0, 0)),
            _const_spec((MXU_COL, MXU_COL)),
        ],
        out_specs=pl.BlockSpec((None, tm, width), lambda g, i: (g, i, 0)),
        out_shape=jax.ShapeDtypeStruct((2, n, width), BF16),
        compiler_params=_cparams(2, VMEM_LIMIT),
        name="qk_proj",
    )(h2d, w_qk, gains, bd)


def _vt_kernel(h_ref, wt_ref, o_ref):
    o_ref[...] = lax.dot_general(wt_ref[...], h_ref[...], _NT,
                                 preferred_element_type=F32).astype(o_ref.dtype)


def _vt_proj(h3d, w_vt, tm):
    b, s, d = h3d.shape
    dout = w_vt.shape[0]
    return pl.pallas_call(
        _vt_kernel,
        grid=(b, s // tm),
        in_specs=[pl.BlockSpec((None, tm, d), lambda i, j: (i, j, 0)),
                  _const_spec((dout, d))],
        out_specs=pl.BlockSpec((None, dout, tm), lambda i, j: (i, 0, j)),
        out_shape=jax.ShapeDtypeStruct((b, dout, s), BF16),
        compiler_params=_cparams(2, VMEM_LIMIT),
        name="vt_proj",
    )(h3d, w_vt)


def _gate_kernel(h_ref, w_ref, o_ref):
    p = jnp.dot(h_ref[...], w_ref[...], preferred_element_type=F32)
    o_ref[...] = (1.0 / (1.0 + jnp.exp(-p))).astype(o_ref.dtype)


def _gate_proj(h2d, w_g, tm):
    n, d = h2d.shape
    width = w_g.shape[2]
    return pl.pallas_call(
        _gate_kernel,
        grid=(2, n // tm),
        in_specs=[pl.BlockSpec((tm, d), lambda g, i: (i, 0)),
                  pl.BlockSpec((None, d, width), lambda g, i: (g, 0, 0))],
        out_specs=pl.BlockSpec((None, tm, width), lambda g, i: (g, i, 0)),
        out_shape=jax.ShapeDtypeStruct((2, n, width), BF16),
        compiler_params=_cparams(2, VMEM_LIMIT),
        name="gate_proj",
    )(h2d, w_g)


def _conv_kernel(h_ref, w_ref, cw_ref, o_ref, carry_ref):
    @pl.when(pl.program_id(1) == 0)
    def _():
        carry_ref[...] = jnp.zeros_like(carry_ref)

    width = o_ref.shape[1]
    tm = o_ref.shape[0]
    p = jnp.dot(h_ref[...], w_ref[...], preferred_element_type=F32)
    y = p[:, width:2 * width] * p[:, 2 * width:]
    rows = lax.broadcasted_iota(I32, y.shape, 0)
    prev2 = carry_ref[SUBLANES - 2:SUBLANES - 1, :]
    prev1 = carry_ref[SUBLANES - 1:SUBLANES, :]
    y1 = jnp.where(rows == 0, prev1, pltpu.roll(y, 1, 0))
    y2 = jnp.where(rows == 0, prev2, jnp.where(rows == 1, prev1, pltpu.roll(y, 2, 0)))
    cw = cw_ref[...]
    z = cw[0:1, :] * y2 + cw[1:2, :] * y1 + cw[2:3, :] * y
    o_ref[...] = (p[:, :width] * z).astype(o_ref.dtype)
    carry_ref[...] = y[tm - SUBLANES:, :]


def _conv_branch(h3d, w_c, conv_w, tm):
    b, s, d = h3d.shape
    width = conv_w.shape[1]
    return pl.pallas_call(
        _conv_kernel,
        grid=(b, s // tm),
        in_specs=[pl.BlockSpec((None, tm, d), lambda i, j: (i, j, 0)),
                  _const_spec((d, 3 * width)),
                  _const_spec((CONV_K, width))],
        out_specs=pl.BlockSpec((None, tm, width), lambda i, j: (i, j, 0)),
        out_shape=jax.ShapeDtypeStruct((b, s, width), BF16),
        scratch_shapes=[pltpu.VMEM((SUBLANES, width), F32)],
        compiler_params=_cparams(2, VMEM_LIMIT),
        name="conv_branch",
    )(h3d, w_c, conv_w)


def _attn_kernel(slopes_ref, q_ref, k_ref, vt_ref, lq1_ref, lk1_ref, lq2_ref, lk2_ref, sg_ref,
                 o_ref, kaug_ref, qaug_ref, m_ref, l_ref, acc_ref, s0_ref, *, tq, cw, lam_init):
    h = pl.program_id(1)
    qi = pl.program_id(2)
    s_len = k_ref.shape[0]
    slope = slopes_ref[h]

    @pl.when(qi == 0)
    def _():
        kpos = lax.broadcasted_iota(I32, (s_len, LANES), 0).astype(F32) * slope
        lane = lax.broadcasted_iota(I32, (s_len, LANES), 1)
        b_hi = kpos.astype(BF16).astype(F32)
        b_lo = kpos - b_hi
        extra = jnp.where(lane == 0, b_hi, jnp.where(lane == 1, b_lo, jnp.where(lane < 4, 1.0, 0.0)))
        kaug_ref[:, :LANES] = k_ref[...]
        kaug_ref[:, LANES:] = extra.astype(BF16)

    q = q_ref[...].astype(F32)
    lane = lax.broadcasted_iota(I32, (tq, LANES), 1)
    c_full = -((jnp.zeros((tq, LANES), I32) + qi * tq).astype(F32) * slope)
    c_hi = c_full.astype(BF16).astype(F32)
    c_lo = c_full - c_hi
    qextra = jnp.where(lane < 2, 1.0, jnp.where(lane == 2, c_hi, jnp.where(lane == 3, c_lo, 0.0)))
    qextra = qextra.astype(BF16)
    qaug_ref[:tq, :LANES] = jnp.where(lane < HEAD_DIM, q, 0.0).astype(BF16)
    qaug_ref[tq:, :LANES] = jnp.where(lane >= HEAD_DIM, q, 0.0).astype(BF16)
    qaug_ref[:tq, LANES:] = qextra
    qaug_ref[tq:, LANES:] = qextra

    m_ref[...] = jnp.full(m_ref.shape, NEG, F32)
    l_ref[...] = jnp.zeros(l_ref.shape, F32)
    acc_ref[...] = jnp.zeros(acc_ref.shape, F32)

    def scores(kt, c, masked):
        lo = c * cw
        q_lo = lo % tq
        kn = q_lo + cw if masked else tq
        k0 = pl.multiple_of(kt * tq, tq)
        s = lax.dot_general(kaug_ref[pl.ds(k0, kn), :], qaug_ref[lo:lo + cw, :], _NT,
                            preferred_element_type=F32)
        if masked:
            kk = lax.broadcasted_iota(I32, s.shape, 0)
            qq = lax.broadcasted_iota(I32, s.shape, 1) + q_lo
            s = jnp.where(kk <= qq, s, NEG)
        return s

    def update(kt, c, s):
        lo = c * cw
        kn = s.shape[0]
        k0 = pl.multiple_of(kt * tq, tq)
        m_old = m_ref[:, lo:lo + cw]
        m_new = jnp.maximum(m_old, jnp.max(s, axis=0, keepdims=True))
        alpha = jnp.exp(m_old - m_new)
        p = jnp.exp(s - m_new)
        l_ref[:, lo:lo + cw] = alpha * l_ref[:, lo:lo + cw] + jnp.sum(p, axis=0, keepdims=True)
        pv = jnp.dot(vt_ref[:, pl.ds(k0, kn)], p.astype(BF16), preferred_element_type=F32)
        acc_ref[:, lo:lo + cw] = alpha * acc_ref[:, lo:lo + cw] + pv
        m_ref[:, lo:lo + cw] = m_new

    def step(kt, masked):
        n_chunks = 2 * tq // cw
        if masked:
            s_next = s0_ref[:cw, :]
            kk = lax.broadcasted_iota(I32, s_next.shape, 0)
            qq = lax.broadcasted_iota(I32, s_next.shape, 1)
            s_next = jnp.where(kk <= qq, s_next, NEG)
        else:
            s_next = s0_ref[...]
        for c in range(n_chunks):
            s_cur = s_next
            if c + 1 < n_chunks:
                s_next = scores(kt, c + 1, masked)
            update(kt, c, s_cur)
        if not masked:
            s0_ref[...] = scores(kt + 1, 0, False)

    def body(kt, carry):
        step(kt, False)
        return carry

    s0_ref[...] = scores(0, 0, False)
    lax.fori_loop(0, qi, body, 0)
    step(qi, True)

    lam = (jnp.exp(jnp.sum(lq1_ref[...] * lk1_ref[...], axis=1, keepdims=True))
           - jnp.exp(jnp.sum(lq2_ref[...] * lk2_ref[...], axis=1, keepdims=True))
           + lam_init)
    o = acc_ref[...] / l_ref[...]
    a = o[:, :tq] - lam * o[:, tq:]
    ms = jnp.mean(a * a, axis=0, keepdims=True)
    a = (a * lax.rsqrt(ms + EPS)) * sg_ref[...]
    a = a * (1.0 - lam_init)
    o_ref[...] = a.T.astype(o_ref.dtype)


def _diff_attention(q3d, k3d, vt3d, slopes, lq1, lk1, lq2, lk2, subln_g, lam_init, tq, cw):
    b, s, width = q3d.shape
    n_heads = width // V_HEAD_DIM
    lam_spec = pl.BlockSpec((1, HEAD_DIM), lambda i, h, j, *_: (0, 0))
    grid_spec = pltpu.PrefetchScalarGridSpec(
        num_scalar_prefetch=1,
        grid=(b, n_heads, s // tq),
        in_specs=[
            pl.BlockSpec((None, tq, V_HEAD_DIM), lambda i, h, j, *_: (i, j, h)),
            pl.BlockSpec((None, s, V_HEAD_DIM), lambda i, h, j, *_: (i, 0, h)),
            pl.BlockSpec((None, V_HEAD_DIM, s), lambda i, h, j, *_: (i, h, 0)),
            lam_spec, lam_spec, lam_spec, lam_spec,
            pl.BlockSpec((V_HEAD_DIM, 1), lambda i, h, j, *_: (0, 0)),
        ],
        out_specs=pl.BlockSpec((None, tq, V_HEAD_DIM), lambda i, h, j, *_: (i, j, h)),
        scratch_shapes=[
            pltpu.VMEM((s, 2 * LANES), BF16),
            pltpu.VMEM((2 * tq, 2 * LANES), BF16),
            pltpu.VMEM((1, 2 * tq), F32),
            pltpu.VMEM((1, 2 * tq), F32),
            pltpu.VMEM((V_HEAD_DIM, 2 * tq), F32),
            pltpu.VMEM((tq, cw), F32),
        ],
    )
    return pl.pallas_call(
        functools.partial(_attn_kernel, tq=tq, cw=cw, lam_init=lam_init),
        grid_spec=grid_spec,
        out_shape=jax.ShapeDtypeStruct((b, s, width), BF16),
        compiler_params=_cparams(3, VMEM_LIMIT),
        name="diff_attention",
    )(slopes, q3d, k3d, vt3d, lq1.reshape(1, -1), lk1.reshape(1, -1), lq2.reshape(1, -1),
      lk2.reshape(1, -1), subln_g.reshape(-1, 1))


def _mix_kernel(x_ref, attn_ref, conv_ref, sga_ref, sgc_ref, wa_ref, wc_ref, wo_ref, gf_ref, wq_ref,
                x1_ref, h2_ref, qp_ref):
    a = jnp.dot(attn_ref[...], wa_ref[...], preferred_element_type=F32)
    c = jnp.dot(conv_ref[...], wc_ref[...], preferred_element_type=F32)
    mixed = sga_ref[...].astype(F32) * a + sgc_ref[...].astype(F32) * c
    x1 = x_ref[...] + jnp.dot(mixed.astype(BF16), wo_ref[...], preferred_element_type=F32)
    ms = jnp.mean(x1 * x1, axis=-1, keepdims=True)
    h2 = (x1 * lax.rsqrt(ms + EPS)) * gf_ref[...]
    for s in range(x1_ref.shape[1]):
        x1_ref[:, s, :] = x1[:, s * LANES:(s + 1) * LANES]
        h2_ref[:, s, :] = h2[:, s * LANES:(s + 1) * LANES]
    qp_ref[...] = jnp.dot(h2.astype(BF16), wq_ref[...], preferred_element_type=F32).astype(qp_ref.dtype)


def _mix(x2d, attn2d, conv2d, sg, wa, wc, wo, gf, wq, tm):
    n, d = x2d.shape
    qw = wq.shape[1]
    row = lambda width: pl.BlockSpec((tm, width), lambda i: (i, 0))
    tiled = pl.BlockSpec((tm, d // LANES, LANES), lambda i: (i, 0, 0))
    return pl.pallas_call(
        _mix_kernel,
        grid=(n // tm,),
        in_specs=[row(d), row(d), row(d),
                  pl.BlockSpec((None, tm, d), lambda i: (0, i, 0)),
                  pl.BlockSpec((None, tm, d), lambda i: (1, i, 0)),
                  _const_spec((d, d)), _const_spec((d, d)), _const_spec((d, d)),
                  _const_spec((1, d)), _const_spec((d, qw))],
        out_specs=[tiled, tiled, row(qw)],
        out_shape=[jax.ShapeDtypeStruct((n, d // LANES, LANES), F32),
                   jax.ShapeDtypeStruct((n, d // LANES, LANES), F32),
                   jax.ShapeDtypeStruct((n, qw), BF16)],
        compiler_params=_cparams(1, VMEM_LIMIT),
        name="mix_residual",
    )(x2d, attn2d, conv2d, sg, sg, wa, wc, wo, gf.reshape(1, d), wq)


def _top_rounds(c, rowid, payload, n_rounds):
    vals, rids, pays = [], [], []
    for _ in range(n_rounds):
        m = jnp.max(c, axis=0, keepdims=True)
        rid = jnp.min(jnp.where(c == m, rowid, 1e9), axis=0, keepdims=True)
        hit = rowid == rid
        vals.append(m)
        rids.append(rid)
        if payload is not None:
            pays.append(jnp.sum(jnp.where(hit, payload, 0.0), axis=0, keepdims=True))
        c = jnp.where(hit, NEG, c)
    return vals, rids, pays


def _stack_rows(rows):
    n, tt = len(rows), rows[0].shape[1]
    rid = lax.broadcasted_iota(I32, (n, tt), 0)
    out = jnp.zeros((n, tt), rows[0].dtype)
    for r, row in enumerate(rows):
        out = jnp.where(rid == r, row, out)
    return out


def _candidates(sv, si):
    assert PEER_TOPK == 2 * SUBLANES
    tt = sv[0].shape[1]
    sub = lax.broadcasted_iota(I32, (SUBLANES, tt), 0)
    subf = sub.astype(F32)
    vals, ids, exps = [], [], []
    for b0 in (0, SUBLANES):
        vals.append(sv[0][0:1, :] + sv[1][b0:b0 + SUBLANES, :])
        ids.append(subf + float(b0))
        exps.append(si[0][0:1, :] * float(N_KEYS) + si[1][b0:b0 + SUBLANES, :])
    for a in range(1, SUBLANES):
        v = sv[0][a:a + 1, :] + sv[1][0:SUBLANES, :]
        vals.append(jnp.where(sub < PEER_TOPK // (a + 1), v, NEG))
        ids.append(subf + float(a * PEER_TOPK))
        exps.append(si[0][a:a + 1, :] * float(N_KEYS) + si[1][0:SUBLANES, :])
    vals.append(sv[0][SUBLANES:, :] + sv[1][0:1, :])
    ids.append((subf + float(SUBLANES)) * float(PEER_TOPK))
    exps.append(si[0][SUBLANES:, :] * float(N_KEYS) + si[1][0:1, :])
    cat = lambda xs: jnp.concatenate(xs, axis=0)
    return cat(vals), cat(ids), cat(exps)


def _topk_kernel(qp_ref, sk_ref, eidx_ref, g_ref, *, row_words):
    tt = qp_ref.shape[0]
    key_id = lax.broadcasted_iota(I32, (N_KEYS, tt), 0).astype(F32)
    e_rows, g_rows = [], []
    for h in range(PEER_HEADS):
        sv, si = [], []
        for p in range(2):
            hp = 2 * h + p
            q = qp_ref[:, hp * D_HALF:(hp + 1) * D_HALF]
            s = lax.dot_general(sk_ref[hp], q, _NT, preferred_element_type=F32)
            vals, rids, _ = _top_rounds(s, key_id, None, PEER_TOPK)
            sv.append(_stack_rows(vals))
            si.append(_stack_rows(rids))
        cand, cand_id, cidx = _candidates(sv, si)
        tv, _, te = _top_rounds(cand, cand_id, cidx, PEER_TOPK)
        tv = _stack_rows(tv)
        ex = jnp.exp(tv - tv[0:1, :])
        g_rows.append(ex / jnp.sum(ex, axis=0, keepdims=True))
        e_rows.append(_stack_rows(te))
    eidx_ref[...] = (jnp.concatenate(e_rows, axis=0).T * float(row_words)).astype(I32)
    g_ref[...] = jnp.concatenate(g_rows, axis=0).T


def _peer_topk(qp, sub_keys, row_words, tt):
    n, qw = qp.shape
    nk = PEER_HEADS * PEER_TOPK
    return pl.pallas_call(
        functools.partial(_topk_kernel, row_words=row_words),
        grid=(n // tt,),
        in_specs=[pl.BlockSpec((tt, qw), lambda i: (i, 0)),
                  _const_spec(sub_keys.shape)],
        out_specs=[pl.BlockSpec((tt, nk), lambda i: (i, 0)),
                   pl.BlockSpec((tt, nk), lambda i: (i, 0))],
        out_shape=[jax.ShapeDtypeStruct((n, nk), I32),
                   jax.ShapeDtypeStruct((n, nk), F32)],
        compiler_params=_cparams(1, VMEM_LIMIT),
        name="peer_topk",
    )(qp, sub_keys)


def _pack_table(t):
    e, d = t.shape
    words = d // (2 * LANES)
    tb = t.astype(BF16).reshape(e * words, 2, LANES)
    return lax.bitcast_convert_type(jnp.swapaxes(tb, 1, 2), I32)


def _gather_rows(idx_row, tab_ref, gbuf_ref, n_sel, words):
    for k in range(n_sel):
        r = pl.multiple_of(idx_row[k], words)
        gbuf_ref[pl.ds(k * words, words), :] = tab_ref[pl.ds(r, words), :]


def _diag_mask(rows, ds, n_sel):
    col = lax.broadcasted_iota(I32, (rows, ds * n_sel), 1)
    row = lax.broadcasted_iota(I32, (rows, ds * n_sel), 0)
    return (col % ds) == (row % ds)


def _gelu(a):
    return 0.5 * a * (1.0 + lax.erf(a * (1.0 / math.sqrt(2.0))))


def _split_bf16(x):
    hi = x.astype(BF16)
    return hi, (x - hi.astype(F32)).astype(BF16)


def _peer_u_kernel(eidx_ref, h_ref, g_ref, u_ref, sel_ref, fold_ref, w_ref, gbuf_ref, dbuf_ref, *, unroll):
    tt, ds, _ = h_ref.shape
    n_sel = g_ref.shape[1]
    words = ds // 2
    mask = _diag_mask(ds, ds, n_sel)

    def group(i, carry):
        for j in range(unroll):
            _gather_rows(eidx_ref.at[i * unroll + j], u_ref, gbuf_ref.at[j], n_sel, words)
        for j in range(unroll):
            t = i * unroll + j
            g2 = pltpu.bitcast(gbuf_ref[j], BF16)
            hs = h_ref[t].astype(BF16)
            r = lax.dot_general(hs, g2, _NT, preferred_element_type=F32)
            dbuf_ref[pl.ds(pl.multiple_of(t * ds, ds), ds), :] = jnp.where(mask, r, 0.0)
        return carry

    lax.fori_loop(0, tt // unroll, group, 0)
    sel = sel_ref[...]
    d_hi, d_lo = _split_bf16(dbuf_ref[...])
    a8 = (jnp.dot(d_hi, sel, preferred_element_type=F32)
          + jnp.dot(d_lo, sel, preferred_element_type=F32))
    a_hi, a_lo = _split_bf16(a8)
    fold = fold_ref[...]
    a = (jnp.dot(fold, a_hi, preferred_element_type=F32)
         + jnp.dot(fold, a_lo, preferred_element_type=F32))
    w_ref[...] = g_ref[...] * _gelu(a)


def _peer_v_kernel(eidx_ref, w_ref, x1_ref, v_ref, selt_ref, rep_ref, o_ref, gbuf_ref, whi_ref, wlo_ref,
                   *, unroll):
    tt, ds, _ = x1_ref.shape
    n_sel = w_ref.shape[1]
    words = ds // 2
    mask = _diag_mask(tt * ds, ds, n_sel)
    selt = selt_ref[...]
    rep = rep_ref[...]
    for part, dst in zip(_split_bf16(w_ref[...]), (whi_ref, wlo_ref)):
        rows = jnp.dot(rep, part, preferred_element_type=F32).astype(BF16)
        dst[...] = jnp.where(mask, jnp.dot(rows, selt, preferred_element_type=F32), 0.0)

    def group(i, carry):
        for j in range(unroll):
            _gather_rows(eidx_ref.at[i * unroll + j], v_ref, gbuf_ref.at[j], n_sel, words)
        for j in range(unroll):
            t = i * unroll + j
            g2 = pltpu.bitcast(gbuf_ref[j], BF16)
            r0 = pl.multiple_of(t * ds, ds)
            o = (jnp.dot(whi_ref[pl.ds(r0, ds), :].astype(BF16), g2, preferred_element_type=F32)
                 + jnp.dot(wlo_ref[pl.ds(r0, ds), :].astype(BF16), g2, preferred_element_type=F32))
            o_ref[t] = x1_ref[t] + o
        return carry

    lax.fori_loop(0, tt // unroll, group, 0)


def _sel_matrix(ds, n_sel):
    grp = np.arange(ds * n_sel) // ds
    return (grp[:, None] == np.arange(n_sel)[None, :]).astype(np.float32)


def _fold_matrix(tt, ds):
    grp = np.arange(tt * ds) // ds
    return (np.arange(tt)[:, None] == grp[None, :]).astype(np.float32)


PEER_UNROLL = 16


def _peer_u(eidx, h3, g, u_packed, tt):
    n, ds, _ = h3.shape
    n_sel = g.shape[1]
    words = ds // 2
    sel = jnp.asarray(_sel_matrix(ds, n_sel), BF16)
    fold = jnp.asarray(_fold_matrix(tt, ds), BF16)
    return pl.pallas_call(
        functools.partial(_peer_u_kernel, unroll=PEER_UNROLL),
        grid=(n // tt,),
        in_specs=[pl.BlockSpec((tt, n_sel), lambda i: (i, 0), memory_space=pltpu.SMEM),
                  pl.BlockSpec((tt, ds, LANES), lambda i: (i, 0, 0)),
                  pl.BlockSpec((tt, n_sel), lambda i: (i, 0)),
                  _const_spec(u_packed.shape),
                  _const_spec(sel.shape),
                  _const_spec(fold.shape)],
        out_specs=pl.BlockSpec((tt, n_sel), lambda i: (i, 0)),
        out_shape=jax.ShapeDtypeStruct((n, n_sel), F32),
        scratch_shapes=[pltpu.VMEM((PEER_UNROLL, n_sel * words, LANES), I32),
                        pltpu.VMEM((tt * ds, n_sel * ds), F32)],
        compiler_params=_cparams(1, VMEM_LIMIT),
        name="peer_u",
    )(eidx, h3, g, u_packed, sel, fold)


def _peer_v(eidx, w, x1_3, v_packed, tt):
    n, ds, _ = x1_3.shape
    n_sel = w.shape[1]
    words = ds // 2
    selt = jnp.asarray(_sel_matrix(ds, n_sel).T, BF16)
    rep = jnp.asarray(_fold_matrix(tt, ds).T, BF16)
    return pl.pallas_call(
        functools.partial(_peer_v_kernel, unroll=PEER_UNROLL),
        grid=(n // tt,),
        in_specs=[pl.BlockSpec((tt, n_sel), lambda i: (i, 0), memory_space=pltpu.SMEM),
                  pl.BlockSpec((tt, n_sel), lambda i: (i, 0)),
                  pl.BlockSpec((tt, ds, LANES), lambda i: (i, 0, 0)),
                  _const_spec(v_packed.shape),
                  _const_spec(selt.shape),
                  _const_spec(rep.shape)],
        out_specs=pl.BlockSpec((tt, ds, LANES), lambda i: (i, 0, 0)),
        out_shape=jax.ShapeDtypeStruct((n, ds, LANES), F32),
        scratch_shapes=[pltpu.VMEM((PEER_UNROLL, n_sel * words, LANES), I32),
                        pltpu.VMEM((tt * ds, n_sel * ds), F32),
                        pltpu.VMEM((tt * ds, n_sel * ds), F32)],
        compiler_params=_cparams(1, VMEM_LIMIT),
        name="peer_v",
    )(eidx, w, x1_3, v_packed, selt, rep)


def _tile(n, pref):
    t = min(n, pref)
    while n % t:
        t //= 2
    return t


def _layer(x, norm_mix_g, w_in, q_norm_g, k_norm_g, lq1, lk1, lq2, lk2, subln_g, w_attn_proj,
           conv_w, w_conv_proj, w_out, norm_ffn_g, w_query, sub_keys, peer_u, peer_v, lam_init):
    b, s, d = x.shape
    n = b * s
    n_heads = d // V_HEAD_DIM
    qk_w = n_heads * 2 * HEAD_DIM
    x2d = x.reshape(n, d)

    wb = w_in.astype(BF16)
    o = 0
    w_qk = jnp.stack([wb[:, 0:qk_w], wb[:, qk_w:2 * qk_w]])
    o = 2 * qk_w
    w_vt = wb[:, o:o + d].T
    o += d
    w_c = wb[:, o:o + 3 * d]
    o += 3 * d
    w_g = jnp.stack([wb[:, o:o + d], wb[:, o + d:o + 2 * d]])
    reps = qk_w // HEAD_DIM
    gains = jnp.stack([jnp.tile(q_norm_g.astype(F32), reps) * (HEAD_DIM ** -0.5),
                       jnp.tile(k_norm_g.astype(F32), reps)]).reshape(2, 1, qk_w)
    slopes = jnp.asarray(np.array([2.0 ** (-8.0 * (i + 1) / n_heads) for i in range(n_heads)],
                                  dtype=np.float32))

    h = _rmsnorm(x2d, norm_mix_g.astype(F32), _tile(n, 1024))
    h3d = h.reshape(b, s, d)
    qk = _qk_proj(h, w_qk, gains, _tile(n, 1024))
    vt = _vt_proj(h3d, w_vt, _tile(s, 512))
    sg = _gate_proj(h, w_g, _tile(n, 1024))
    convb = _conv_branch(h3d, w_c, conv_w.astype(F32), _tile(s, 512))
    attn = _diff_attention(qk[0].reshape(b, s, qk_w), qk[1].reshape(b, s, qk_w), vt, slopes,
                           lq1.astype(F32), lk1.astype(F32), lq2.astype(F32), lk2.astype(F32),
                           subln_g.astype(F32), lam_init, _tile(s, 512), MXU_COL)
    x1, h2, qp = _mix(x2d, attn.reshape(n, d), convb.reshape(n, d), sg,
                      w_attn_proj.astype(BF16), w_conv_proj.astype(BF16), w_out.astype(BF16),
                      norm_ffn_g.astype(F32), w_query.astype(BF16), _tile(n, 512))

    skb = sub_keys.astype(BF16).reshape(PEER_HEADS * 2, N_KEYS, D_HALF)
    ds = d // LANES
    eidx, g = _peer_topk(qp, skb, ds // 2, _tile(n, 256))
    tt = _tile(n, 128)
    w = _peer_u(eidx, h2, g, _pack_table(peer_u), tt)
    out = _peer_v(eidx, w, x1, _pack_table(peer_v), tt)
    return out.reshape(b, s, d)


def kernel(x, norm_mix_g, w_in, q_norm_g, k_norm_g, lambda_q1, lambda_k1, lambda_q2, lambda_k2, subln_g, w_attn_proj, conv_w, w_conv_proj, w_out, norm_ffn_g, peer_w_query, peer_sub_keys, peer_u, peer_v):
    depth = w_in.shape[0]
    for l in range(depth):
        lam_init = 0.8 - 0.6 * math.exp(-0.3 * l)
        x = _layer(x, norm_mix_g[l], w_in[l], q_norm_g[l], k_norm_g[l], lambda_q1[l], lambda_k1[l],
                   lambda_q2[l], lambda_k2[l], subln_g[l], w_attn_proj[l], conv_w[l], w_conv_proj[l],
                   w_out[l], norm_ffn_g[l], peer_w_query[l], peer_sub_keys[l], peer_u[l], peer_v[l],
                   lam_init)
    return x
```

```python
import functools
import math

import numpy as np
import jax
import jax.numpy as jnp
from jax import lax
from jax.experimental import pallas as pl
from jax.experimental.pallas import tpu as pltpu

F32 = jnp.float32
BF16 = jnp.bfloat16
I32 = jnp.int32

EPS = 1e-6
HEAD_DIM = 64
V_HEAD_DIM = 2 * HEAD_DIM
CONV_K = 3
PEER_HEADS = 8
N_KEYS = 128
PEER_TOPK = 16
D_HALF = 128
LANES = 128
SUBLANES = 8
MXU_COL = 256
NEG = -1e30
VMEM_LIMIT = 56 * 1024 * 1024

_NT = (((1,), (1,)), ((), ()))


def _cparams(n_axes, vmem=None):
    return pltpu.CompilerParams(
        dimension_semantics=("arbitrary",) * n_axes,
        vmem_limit_bytes=vmem)


def _const_spec(shape):
    nd = len(shape)
    return pl.BlockSpec(shape, lambda *_: (0,) * nd, pipeline_mode=pl.Buffered(1))


def _rmsnorm_kernel(x_ref, g_ref, o_ref):
    x = x_ref[...]
    ms = jnp.mean(x * x, axis=-1, keepdims=True)
    o_ref[...] = ((x * lax.rsqrt(ms + EPS)) * g_ref[...]).astype(o_ref.dtype)


def _rmsnorm(x2d, g, tm):
    n, d = x2d.shape
    return pl.pallas_call(
        _rmsnorm_kernel,
        grid=(n // tm,),
        in_specs=[pl.BlockSpec((tm, d), lambda i: (i, 0)), _const_spec((1, d))],
        out_specs=pl.BlockSpec((tm, d), lambda i: (i, 0)),
        out_shape=jax.ShapeDtypeStruct((n, d), BF16),
        compiler_params=_cparams(1),
        name="rmsnorm_in",
    )(x2d, g.reshape(1, d))


def _qk_kernel(h_ref, w_ref, g_ref, bd_ref, o_ref):
    p = jnp.dot(h_ref[...], w_ref[...], preferred_element_type=F32)
    sq = p * p
    hi = sq.astype(BF16)
    lo = (sq - hi.astype(F32)).astype(BF16)
    bd = bd_ref[...]
    parts = []
    for c in range(p.shape[1] // MXU_COL):
        sl = slice(c * MXU_COL, (c + 1) * MXU_COL)
        parts.append(jnp.dot(hi[:, sl], bd, preferred_element_type=F32)
                     + jnp.dot(lo[:, sl], bd, preferred_element_type=F32))
    ss = jnp.concatenate(parts, axis=1)
    y = p * lax.rsqrt(ss * (1.0 / HEAD_DIM) + EPS)
    o_ref[...] = (y * g_ref[...]).astype(o_ref.dtype)


def _qk_proj(h2d, w_qk, gains, tm):
    n, d = h2d.shape
    width = w_qk.shape[2]
    grp = np.arange(MXU_COL) // HEAD_DIM
    bd = jnp.asarray((grp[:, None] == grp[None, :]).astype(np.float32), BF16)
    return pl.pallas_call(
        _qk_kernel,
        grid=(2, n // tm),
        in_specs=[
            pl.BlockSpec((tm, d), lambda g, i: (i, 0)),
            pl.BlockSpec((None, d, width), lambda g, i: (g, 0, 0)),
            pl.BlockSpec((None, 1, width), lambda g, i: (g, 0, 0)),
            _const_spec((MXU_COL, MXU_COL)),
        ],
        out_specs=pl.BlockSpec((None, tm, width), lambda g, i: (g, i, 0)),
        out_shape=jax.ShapeDtypeStruct((2, n, width), BF16),
        compiler_params=_cparams(2, VMEM_LIMIT),
        name="qk_proj",
    )(h2d, w_qk, gains, bd)


def _vt_kernel(h_ref, wt_ref, o_ref):
    o_ref[...] = lax.dot_general(wt_ref[...], h_ref[...], _NT,
                                 preferred_element_type=F32).astype(o_ref.dtype)


def _vt_proj(h3d, w_vt, tm):
    b, s, d = h3d.shape
    dout = w_vt.shape[0]
    return pl.pallas_call(
        _vt_kernel,
        grid=(b, s // tm),
        in_specs=[pl.BlockSpec((None, tm, d), lambda i, j: (i, j, 0)),
                  _const_spec((dout, d))],
        out_specs=pl.BlockSpec((None, dout, tm), lambda i, j: (i, 0, j)),
        out_shape=jax.ShapeDtypeStruct((b, dout, s), BF16),
        compiler_params=_cparams(2, VMEM_LIMIT),
        name="vt_proj",
    )(h3d, w_vt)


def _gate_kernel(h_ref, w_ref, o_ref):
    p = jnp.dot(h_ref[...], w_ref[...], preferred_element_type=F32)
    o_ref[...] = (1.0 / (1.0 + jnp.exp(-p))).astype(o_ref.dtype)


def _gate_proj(h2d, w_g, tm):
    n, d = h2d.shape
    width = w_g.shape[2]
    return pl.pallas_call(
        _gate_kernel,
        grid=(2, n // tm),
        in_specs=[pl.BlockSpec((tm, d), lambda g, i: (i, 0)),
                  pl.BlockSpec((None, d, width), lambda g, i: (g, 0, 0))],
        out_specs=pl.BlockSpec((None, tm, width), lambda g, i: (g, i, 0)),
        out_shape=jax.ShapeDtypeStruct((2, n, width), BF16),
        compiler_params=_cparams(2, VMEM_LIMIT),
        name="gate_proj",
    )(h2d, w_g)


def _conv_kernel(h_ref, w_ref, cw_ref, o_ref, carry_ref):
    @pl.when(pl.program_id(1) == 0)
    def _():
        carry_ref[...] = jnp.zeros_like(carry_ref)

    width = o_ref.shape[1]
    tm = o_ref.shape[0]
    p = jnp.dot(h_ref[...], w_ref[...], preferred_element_type=F32)
    y = p[:, width:2 * width] * p[:, 2 * width:]
    rows = lax.broadcasted_iota(I32, y.shape, 0)
    prev2 = carry_ref[SUBLANES - 2:SUBLANES - 1, :]
    prev1 = carry_ref[SUBLANES - 1:SUBLANES, :]
    y1 = jnp.where(rows == 0, prev1, pltpu.roll(y, 1, 0))
    y2 = jnp.where(rows == 0, prev2, jnp.where(rows == 1, prev1, pltpu.roll(y, 2, 0)))
    cw = cw_ref[...]
    z = cw[0:1, :] * y2 + cw[1:2, :] * y1 + cw[2:3, :] * y
    o_ref[...] = (p[:, :width] * z).astype(o_ref.dtype)
    carry_ref[...] = y[tm - SUBLANES:, :]


def _conv_branch(h3d, w_c, conv_w, tm):
    b, s, d = h3d.shape
    width = conv_w.shape[1]
    return pl.pallas_call(
        _conv_kernel,
        grid=(b, s // tm),
        in_specs=[pl.BlockSpec((None, tm, d), lambda i, j: (i, j, 0)),
                  _const_spec((d, 3 * width)),
                  _const_spec((CONV_K, width))],
        out_specs=pl.BlockSpec((None, tm, width), lambda i, j: (i, j, 0)),
        out_shape=jax.ShapeDtypeStruct((b, s, width), BF16),
        scratch_shapes=[pltpu.VMEM((SUBLANES, width), F32)],
        compiler_params=_cparams(2, VMEM_LIMIT),
        name="conv_branch",
    )(h3d, w_c, conv_w)


def _attn_kernel(slopes_ref, q_ref, k_ref, vt_ref, lq1_ref, lk1_ref, lq2_ref, lk2_ref, sg_ref,
                 o_ref, kaug_ref, qaug_ref, m_ref, l_ref, acc_ref, s0_ref, *, tq, cw, lam_init):
    h = pl.program_id(1)
    qi = pl.program_id(2)
    s_len = k_ref.shape[0]
    slope = slopes_ref[h]

    @pl.when(qi == 0)
    def _():
        kpos = lax.broadcasted_iota(I32, (s_len, LANES), 0).astype(F32) * slope
        lane = lax.broadcasted_iota(I32, (s_len, LANES), 1)
        b_hi = kpos.astype(BF16).astype(F32)
        b_lo = kpos - b_hi
        extra = jnp.where(lane == 0, b_hi, jnp.where(lane == 1, b_lo, jnp.where(lane < 4, 1.0, 0.0)))
        kaug_ref[:, :LANES] = k_ref[...]
        kaug_ref[:, LANES:] = extra.astype(BF16)

    q = q_ref[...].astype(F32)
    lane = lax.broadcasted_iota(I32, (tq, LANES), 1)
    c_full = -((jnp.zeros((tq, LANES), I32) + qi * tq).astype(F32) * slope)
    c_hi = c_full.astype(BF16).astype(F32)
    c_lo = c_full - c_hi
    qextra = jnp.where(lane < 2, 1.0, jnp.where(lane == 2, c_hi, jnp.where(lane == 3, c_lo, 0.0)))
    qextra = qextra.astype(BF16)
    qaug_ref[:tq, :LANES] = jnp.where(lane < HEAD_DIM, q, 0.0).astype(BF16)
    qaug_ref[tq:, :LANES] = jnp.where(lane >= HEAD_DIM, q, 0.0).astype(BF16)
    qaug_ref[:tq, LANES:] = qextra
    qaug_ref[tq:, LANES:] = qextra

    m_ref[...] = jnp.full(m_ref.shape, NEG, F32)
    l_ref[...] = jnp.zeros(l_ref.shape, F32)
    acc_ref[...] = jnp.zeros(acc_ref.shape, F32)

    def scores(kt, c, masked):
        lo = c * cw
        q_lo = lo % tq
        kn = q_lo + cw if masked else tq
        k0 = pl.multiple_of(kt * tq, tq)
        s = lax.dot_general(kaug_ref[pl.ds(k0, kn), :], qaug_ref[lo:lo + cw, :], _NT,
                            preferred_element_type=F32)
        if masked:
            kk = lax.broadcasted_iota(I32, s.shape, 0)
            qq = lax.broadcasted_iota(I32, s.shape, 1) + q_lo
            s = jnp.where(kk <= qq, s, NEG)
        return s

    def update(kt, c, s):
        lo = c * cw
        kn = s.shape[0]
        k0 = pl.multiple_of(kt * tq, tq)
        m_old = m_ref[:, lo:lo + cw]
        m_new = jnp.maximum(m_old, jnp.max(s, axis=0, keepdims=True))
        alpha = jnp.exp(m_old - m_new)
        p = jnp.exp(s - m_new)
        l_ref[:, lo:lo + cw] = alpha * l_ref[:, lo:lo + cw] + jnp.sum(p, axis=0, keepdims=True)
        pv = jnp.dot(vt_ref[:, pl.ds(k0, kn)], p.astype(BF16), preferred_element_type=F32)
        acc_ref[:, lo:lo + cw] = alpha * acc_ref[:, lo:lo + cw] + pv
        m_ref[:, lo:lo + cw] = m_new

    def step(kt, masked):
        n_chunks = 2 * tq // cw
        if masked:
            s_next = s0_ref[:cw, :]
            kk = lax.broadcasted_iota(I32, s_next.shape, 0)
            qq = lax.broadcasted_iota(I32, s_next.shape, 1)
            s_next = jnp.where(kk <= qq, s_next, NEG)
        else:
            s_next = s0_ref[...]
        for c in range(n_chunks):
            s_cur = s_next
            if c + 1 < n_chunks:
                s_next = scores(kt, c + 1, masked)
            update(kt, c, s_cur)
        if not masked:
            s0_ref[...] = scores(kt + 1, 0, False)

    def body(kt, carry):
        step(kt, False)
        return carry

    s0_ref[...] = scores(0, 0, False)
    lax.fori_loop(0, qi, body, 0)
    step(qi, True)

    lam = (jnp.exp(jnp.sum(lq1_ref[...] * lk1_ref[...], axis=1, keepdims=True))
           - jnp.exp(jnp.sum(lq2_ref[...] * lk2_ref[...], axis=1, keepdims=True))
           + lam_init)
    o = acc_ref[...] / l_ref[...]
    a = o[:, :tq] - lam * o[:, tq:]
    ms = jnp.mean(a * a, axis=0, keepdims=True)
    a = (a * lax.rsqrt(ms + EPS)) * sg_ref[...]
    a = a * (1.0 - lam_init)
    o_ref[...] = a.T.astype(o_ref.dtype)


def _diff_attention(q3d, k3d, vt3d, slopes, lq1, lk1, lq2, lk2, subln_g, lam_init, tq, cw):
    b, s, width = q3d.shape
    n_heads = width // V_HEAD_DIM
    lam_spec = pl.BlockSpec((1, HEAD_DIM), lambda i, h, j, *_: (0, 0))
    grid_spec = pltpu.PrefetchScalarGridSpec(
        num_scalar_prefetch=1,
        grid=(b, n_heads, s // tq),
        in_specs=[
            pl.BlockSpec((None, tq, V_HEAD_DIM), lambda i, h, j, *_: (i, j, h)),
            pl.BlockSpec((None, s, V_HEAD_DIM), lambda i, h, j, *_: (i, 0, h)),
            pl.BlockSpec((None, V_HEAD_DIM, s), lambda i, h, j, *_: (i, h, 0)),
            lam_spec, lam_spec, lam_spec, lam_spec,
            pl.BlockSpec((V_HEAD_DIM, 1), lambda i, h, j, *_: (0, 0)),
        ],
        out_specs=pl.BlockSpec((None, tq, V_HEAD_DIM), lambda i, h, j, *_: (i, j, h)),
        scratch_shapes=[
            pltpu.VMEM((s, 2 * LANES), BF16),
            pltpu.VMEM((2 * tq, 2 * LANES), BF16),
            pltpu.VMEM((1, 2 * tq), F32),
            pltpu.VMEM((1, 2 * tq), F32),
            pltpu.VMEM((V_HEAD_DIM, 2 * tq), F32),
            pltpu.VMEM((tq, cw), F32),
        ],
    )
    return pl.pallas_call(
        functools.partial(_attn_kernel, tq=tq, cw=cw, lam_init=lam_init),
        grid_spec=grid_spec,
        out_shape=jax.ShapeDtypeStruct((b, s, width), BF16),
        compiler_params=_cparams(3, VMEM_LIMIT),
        name="diff_attention",
    )(slopes, q3d, k3d, vt3d, lq1.reshape(1, -1), lk1.reshape(1, -1), lq2.reshape(1, -1),
      lk2.reshape(1, -1), subln_g.reshape(-1, 1))


def _mix_kernel(x_ref, attn_ref, conv_ref, sga_ref, sgc_ref, wa_ref, wc_ref, wo_ref, gf_ref, wq_ref,
                x1_ref, h2_ref, qp_ref):
    a = jnp.dot(attn_ref[...], wa_ref[...], preferred_element_type=F32)
    c = jnp.dot(conv_ref[...], wc_ref[...], preferred_element_type=F32)
    mixed = sga_ref[...].astype(F32) * a + sgc_ref[...].astype(F32) * c
    x1 = x_ref[...] + jnp.dot(mixed.astype(BF16), wo_ref[...], preferred_element_type=F32)
    ms = jnp.mean(x1 * x1, axis=-1, keepdims=True)
    h2 = (x1 * lax.rsqrt(ms + EPS)) * gf_ref[...]
    for s in range(x1_ref.shape[1]):
        x1_ref[:, s, :] = x1[:, s * LANES:(s + 1) * LANES]
        h2_ref[:, s, :] = h2[:, s * LANES:(s + 1) * LANES]
    qp_ref[...] = jnp.dot(h2.astype(BF16), wq_ref[...], preferred_element_type=F32).astype(qp_ref.dtype)


def _mix(x2d, attn2d, conv2d, sg, wa, wc, wo, gf, wq, tm):
    n, d = x2d.shape
    qw = wq.shape[1]
    row = lambda width: pl.BlockSpec((tm, width), lambda i: (i, 0))
    tiled = pl.BlockSpec((tm, d // LANES, LANES), lambda i: (i, 0, 0))
    return pl.pallas_call(
        _mix_kernel,
        grid=(n // tm,),
        in_specs=[row(d), row(d), row(d),
                  pl.BlockSpec((None, tm, d), lambda i: (0, i, 0)),
                  pl.BlockSpec((None, tm, d), lambda i: (1, i, 0)),
                  _const_spec((d, d)), _const_spec((d, d)), _const_spec((d, d)),
                  _const_spec((1, d)), _const_spec((d, qw))],
        out_specs=[tiled, tiled, row(qw)],
        out_shape=[jax.ShapeDtypeStruct((n, d // LANES, LANES), F32),
                   jax.ShapeDtypeStruct((n, d // LANES, LANES), F32),
                   jax.ShapeDtypeStruct((n, qw), BF16)],
        compiler_params=_cparams(1, VMEM_LIMIT),
        name="mix_residual",
    )(x2d, attn2d, conv2d, sg, sg, wa, wc, wo, gf.reshape(1, d), wq)


def _top_rounds(c, rowid, payload, n_rounds):
    vals, rids, pays = [], [], []
    for _ in range(n_rounds):
        m = jnp.max(c, axis=0, keepdims=True)
        rid = jnp.min(jnp.where(c == m, rowid, 1e9), axis=0, keepdims=True)
        hit = rowid == rid
        vals.append(m)
        rids.append(rid)
        if payload is not None:
            pays.append(jnp.sum(jnp.where(hit, payload, 0.0), axis=0, keepdims=True))
        c = jnp.where(hit, NEG, c)
    return vals, rids, pays


def _stack_rows(rows):
    n, tt = len(rows), rows[0].shape[1]
    rid = lax.broadcasted_iota(I32, (n, tt), 0)
    out = jnp.zeros((n, tt), rows[0].dtype)
    for r, row in enumerate(rows):
        out = jnp.where(rid == r, row, out)
    return out


def _candidates(sv, si):
    assert PEER_TOPK == 2 * SUBLANES
    tt = sv[0].shape[1]
    sub = lax.broadcasted_iota(I32, (SUBLANES, tt), 0)
    subf = sub.astype(F32)
    vals, ids, exps = [], [], []
    for b0 in (0, SUBLANES):
        vals.append(sv[0][0:1, :] + sv[1][b0:b0 + SUBLANES, :])
        ids.append(subf + float(b0))
        exps.append(si[0][0:1, :] * float(N_KEYS) + si[1][b0:b0 + SUBLANES, :])
    for a in range(1, SUBLANES):
        v = sv[0][a:a + 1, :] + sv[1][0:SUBLANES, :]
        vals.append(jnp.where(sub < PEER_TOPK // (a + 1), v, NEG))
        ids.append(subf + float(a * PEER_TOPK))
        exps.append(si[0][a:a + 1, :] * float(N_KEYS) + si[1][0:SUBLANES, :])
    vals.append(sv[0][SUBLANES:, :] + sv[1][0:1, :])
    ids.append((subf + float(SUBLANES)) * float(PEER_TOPK))
    exps.append(si[0][SUBLANES:, :] * float(N_KEYS) + si[1][0:1, :])
    cat = lambda xs: jnp.concatenate(xs, axis=0)
    return cat(vals), cat(ids), cat(exps)


def _topk_kernel(qp_ref, sk_ref, eidx_ref, g_ref, *, row_words):
    tt = qp_ref.shape[0]
    key_id = lax.broadcasted_iota(I32, (N_KEYS, tt), 0).astype(F32)
    e_rows, g_rows = [], []
    for h in range(PEER_HEADS):
        sv, si = [], []
        for p in range(2):
            hp = 2 * h + p
            q = qp_ref[:, hp * D_HALF:(hp + 1) * D_HALF]
            s = lax.dot_general(sk_ref[hp], q, _NT, preferred_element_type=F32)
            vals, rids, _ = _top_rounds(s, key_id, None, PEER_TOPK)
            sv.append(_stack_rows(vals))
            si.append(_stack_rows(rids))
        cand, cand_id, cidx = _candidates(sv, si)
        tv, _, te = _top_rounds(cand, cand_id, cidx, PEER_TOPK)
        tv = _stack_rows(tv)
        ex = jnp.exp(tv - tv[0:1, :])
        g_rows.append(ex / jnp.sum(ex, axis=0, keepdims=True))
        e_rows.append(_stack_rows(te))
    eidx_ref[...] = (jnp.concatenate(e_rows, axis=0).T * float(row_words)).astype(I32)
    g_ref[...] = jnp.concatenate(g_rows, axis=0).T


def _peer_topk(qp, sub_keys, row_words, tt):
    n, qw = qp.shape
    nk = PEER_HEADS * PEER_TOPK
    return pl.pallas_call(
        functools.partial(_topk_kernel, row_words=row_words),
        grid=(n // tt,),
        in_specs=[pl.BlockSpec((tt, qw), lambda i: (i, 0)),
                  _const_spec(sub_keys.shape)],
        out_specs=[pl.BlockSpec((tt, nk), lambda i: (i, 0)),
                   pl.BlockSpec((tt, nk), lambda i: (i, 0))],
        out_shape=[jax.ShapeDtypeStruct((n, nk), I32),
                   jax.ShapeDtypeStruct((n, nk), F32)],
        compiler_params=_cparams(1, VMEM_LIMIT),
        name="peer_topk",
    )(qp, sub_keys)


def _pack_kernel(t_ref, o_ref, *, words):
    rows = t_ref.shape[0]
    for j in range(words):
        lo = t_ref[:, (2 * j) * LANES:(2 * j + 1) * LANES].astype(BF16).astype(F32)
        hi = t_ref[:, (2 * j + 1) * LANES:(2 * j + 2) * LANES].astype(BF16).astype(F32)
        w = (pltpu.bitcast(lo, jnp.uint32) >> 16) | (pltpu.bitcast(hi, jnp.uint32) & jnp.uint32(0xFFFF0000))
        o_ref[pl.ds(j, rows, stride=words), :] = pltpu.bitcast(w, I32)


def _pack_table(t, rows=512):
    e, d = t.shape
    words = d // (2 * LANES)
    rows = _tile(e, rows)
    return pl.pallas_call(
        functools.partial(_pack_kernel, words=words),
        grid=(e // rows,),
        in_specs=[pl.BlockSpec((rows, d), lambda i: (i, 0))],
        out_specs=pl.BlockSpec((rows * words, LANES), lambda i: (i, 0)),
        out_shape=jax.ShapeDtypeStruct((e * words, LANES), I32),
        compiler_params=_cparams(1),
        name="pack_table",
    )(t)


def _gather_rows(idx_row, tab_ref, gbuf_ref, n_sel, words):
    for k in range(n_sel):
        r = pl.multiple_of(idx_row[k], words)
        gbuf_ref[pl.ds(k * words, words), :] = tab_ref[pl.ds(r, words), :]


def _diag_mask(rows, ds, n_sel):
    col = lax.broadcasted_iota(I32, (rows, ds * n_sel), 1)
    row = lax.broadcasted_iota(I32, (rows, ds * n_sel), 0)
    return (col % ds) == (row % ds)


def _gelu(a):
    return 0.5 * a * (1.0 + lax.erf(a * (1.0 / math.sqrt(2.0))))


def _split_bf16(x):
    hi = x.astype(BF16)
    return hi, (x - hi.astype(F32)).astype(BF16)


def _peer_u_kernel(eidx_ref, h_ref, g_ref, u_ref, sel_ref, fold_ref, w_ref, gbuf_ref, dbuf_ref, *, unroll):
    tt, ds, _ = h_ref.shape
    n_sel = g_ref.shape[1]
    words = ds // 2
    mask = _diag_mask(ds, ds, n_sel)

    def group(i, carry):
        for j in range(unroll):
            _gather_rows(eidx_ref.at[i * unroll + j], u_ref, gbuf_ref.at[j], n_sel, words)
        for j in range(unroll):
            t = i * unroll + j
            g2 = pltpu.bitcast(gbuf_ref[j], BF16)
            hs = h_ref[t].astype(BF16)
            r = lax.dot_general(hs, g2, _NT, preferred_element_type=F32)
            dbuf_ref[pl.ds(pl.multiple_of(t * ds, ds), ds), :] = jnp.where(mask, r, 0.0)
        return carry

    lax.fori_loop(0, tt // unroll, group, 0)
    sel = sel_ref[...]
    d_hi, d_lo = _split_bf16(dbuf_ref[...])
    a8 = (jnp.dot(d_hi, sel, preferred_element_type=F32)
          + jnp.dot(d_lo, sel, preferred_element_type=F32))
    a_hi, a_lo = _split_bf16(a8)
    fold = fold_ref[...]
    a = (jnp.dot(fold, a_hi, preferred_element_type=F32)
         + jnp.dot(fold, a_lo, preferred_element_type=F32))
    w_ref[...] = g_ref[...] * _gelu(a)


def _peer_v_kernel(eidx_ref, w_ref, x1_ref, v_ref, selt_ref, rep_ref, o_ref, gbuf_ref, whi_ref, wlo_ref,
                   mask_ref, *, unroll):
    tt, ds, _ = x1_ref.shape
    n_sel = w_ref.shape[1]
    words = ds // 2
    @pl.when(pl.program_id(0) == 0)
    def _():
        mask_ref[...] = _diag_mask(tt * ds, ds, n_sel).astype(F32)

    selt = selt_ref[...]
    rep = rep_ref[...]
    for part, dst in zip(_split_bf16(w_ref[...]), (whi_ref, wlo_ref)):
        rows = jnp.dot(rep, part, preferred_element_type=F32).astype(BF16)
        dst[...] = mask_ref[...] * jnp.dot(rows, selt, preferred_element_type=F32)

    def group(i, carry):
        for j in range(unroll):
            _gather_rows(eidx_ref.at[i * unroll + j], v_ref, gbuf_ref.at[j], n_sel, words)
        for j in range(unroll):
            t = i * unroll + j
            g2 = pltpu.bitcast(gbuf_ref[j], BF16)
            r0 = pl.multiple_of(t * ds, ds)
            o = (jnp.dot(whi_ref[pl.ds(r0, ds), :].astype(BF16), g2, preferred_element_type=F32)
                 + jnp.dot(wlo_ref[pl.ds(r0, ds), :].astype(BF16), g2, preferred_element_type=F32))
            o_ref[t] = x1_ref[t] + o
        return carry

    lax.fori_loop(0, tt // unroll, group, 0)


def _sel_matrix(ds, n_sel):
    grp = np.arange(ds * n_sel) // ds
    return (grp[:, None] == np.arange(n_sel)[None, :]).astype(np.float32)


def _fold_matrix(tt, ds):
    grp = np.arange(tt * ds) // ds
    return (np.arange(tt)[:, None] == grp[None, :]).astype(np.float32)


PEER_UNROLL = 16


def _peer_u(eidx, h3, g, u_packed, tt):
    n, ds, _ = h3.shape
    n_sel = g.shape[1]
    words = ds // 2
    sel = jnp.asarray(_sel_matrix(ds, n_sel), BF16)
    fold = jnp.asarray(_fold_matrix(tt, ds), BF16)
    return pl.pallas_call(
        functools.partial(_peer_u_kernel, unroll=PEER_UNROLL),
        grid=(n // tt,),
        in_specs=[pl.BlockSpec((tt, n_sel), lambda i: (i, 0), memory_space=pltpu.SMEM),
                  pl.BlockSpec((tt, ds, LANES), lambda i: (i, 0, 0)),
                  pl.BlockSpec((tt, n_sel), lambda i: (i, 0)),
                  _const_spec(u_packed.shape),
                  _const_spec(sel.shape),
                  _const_spec(fold.shape)],
        out_specs=pl.BlockSpec((tt, n_sel), lambda i: (i, 0)),
        out_shape=jax.ShapeDtypeStruct((n, n_sel), F32),
        scratch_shapes=[pltpu.VMEM((PEER_UNROLL, n_sel * words, LANES), I32),
                        pltpu.VMEM((tt * ds, n_sel * ds), F32)],
        compiler_params=_cparams(1, VMEM_LIMIT),
        name="peer_u",
    )(eidx, h3, g, u_packed, sel, fold)


def _peer_v(eidx, w, x1_3, v_packed, tt):
    n, ds, _ = x1_3.shape
    n_sel = w.shape[1]
    words = ds // 2
    selt = jnp.asarray(_sel_matrix(ds, n_sel).T, BF16)
    rep = jnp.asarray(_fold_matrix(tt, ds).T, BF16)
    return pl.pallas_call(
        functools.partial(_peer_v_kernel, unroll=PEER_UNROLL),
        grid=(n // tt,),
        in_specs=[pl.BlockSpec((tt, n_sel), lambda i: (i, 0), memory_space=pltpu.SMEM),
                  pl.BlockSpec((tt, n_sel), lambda i: (i, 0)),
                  pl.BlockSpec((tt, ds, LANES), lambda i: (i, 0, 0)),
                  _const_spec(v_packed.shape),
                  _const_spec(selt.shape),
                  _const_spec(rep.shape)],
        out_specs=pl.BlockSpec((tt, ds, LANES), lambda i: (i, 0, 0)),
        out_shape=jax.ShapeDtypeStruct((n, ds, LANES), F32),
        scratch_shapes=[pltpu.VMEM((PEER_UNROLL, n_sel * words, LANES), I32),
                        pltpu.VMEM((tt * ds, n_sel * ds), F32),
                        pltpu.VMEM((tt * ds, n_sel * ds), F32),
                        pltpu.VMEM((tt * ds, n_sel * ds), F32)],
        compiler_params=_cparams(1, VMEM_LIMIT),
        name="peer_v",
    )(eidx, w, x1_3, v_packed, selt, rep)


def _tile(n, pref):
    t = min(n, pref)
    while n % t:
        t //= 2
    return t


def _layer(x, norm_mix_g, w_in, q_norm_g, k_norm_g, lq1, lk1, lq2, lk2, subln_g, w_attn_proj,
           conv_w, w_conv_proj, w_out, norm_ffn_g, w_query, sub_keys, peer_u, peer_v, lam_init):
    b, s, d = x.shape
    n = b * s
    n_heads = d // V_HEAD_DIM
    qk_w = n_heads * 2 * HEAD_DIM
    x2d = x.reshape(n, d)

    wb = w_in.astype(BF16)
    o = 0
    w_qk = jnp.stack([wb[:, 0:qk_w], wb[:, qk_w:2 * qk_w]])
    o = 2 * qk_w
    w_vt = wb[:, o:o + d].T
    o += d
    w_c = wb[:, o:o + 3 * d]
    o += 3 * d
    w_g = jnp.stack([wb[:, o:o + d], wb[:, o + d:o + 2 * d]])
    reps = qk_w // HEAD_DIM
    gains = jnp.stack([jnp.tile(q_norm_g.astype(F32), reps) * (HEAD_DIM ** -0.5),
                       jnp.tile(k_norm_g.astype(F32), reps)]).reshape(2, 1, qk_w)
    slopes = jnp.asarray(np.array([2.0 ** (-8.0 * (i + 1) / n_heads) for i in range(n_heads)],
                                  dtype=np.float32))

    h = _rmsnorm(x2d, norm_mix_g.astype(F32), _tile(n, 1024))
    h3d = h.reshape(b, s, d)
    qk = _qk_proj(h, w_qk, gains, _tile(n, 1024))
    vt = _vt_proj(h3d, w_vt, _tile(s, 512))
    sg = _gate_proj(h, w_g, _tile(n, 1024))
    convb = _conv_branch(h3d, w_c, conv_w.astype(F32), _tile(s, 512))
    attn = _diff_attention(qk[0].reshape(b, s, qk_w), qk[1].reshape(b, s, qk_w), vt, slopes,
                           lq1.astype(F32), lk1.astype(F32), lq2.astype(F32), lk2.astype(F32),
                           subln_g.astype(F32), lam_init, _tile(s, 512), 2 * MXU_COL)
    x1, h2, qp = _mix(x2d, attn.reshape(n, d), convb.reshape(n, d), sg,
                      w_attn_proj.astype(BF16), w_conv_proj.astype(BF16), w_out.astype(BF16),
                      norm_ffn_g.astype(F32), w_query.astype(BF16), _tile(n, 512))

    skb = sub_keys.astype(BF16).reshape(PEER_HEADS * 2, N_KEYS, D_HALF)
    ds = d // LANES
    eidx, g = _peer_topk(qp, skb, ds // 2, _tile(n, 256))
    tt = _tile(n, 128)
    w = _peer_u(eidx, h2, g, _pack_table(peer_u), tt)
    out = _peer_v(eidx, w, x1, _pack_table(peer_v), tt)
    return out.reshape(b, s, d)


def kernel(x, norm_mix_g, w_in, q_norm_g, k_norm_g, lambda_q1, lambda_k1, lambda_q2, lambda_k2, subln_g, w_attn_proj, conv_w, w_conv_proj, w_out, norm_ffn_g, peer_w_query, peer_sub_keys, peer_u, peer_v):
    depth = w_in.shape[0]
    for l in range(depth):
        lam_init = 0.8 - 0.6 * math.exp(-0.3 * l)
        x = _layer(x, norm_mix_g[l], w_in[l], q_norm_g[l], k_norm_g[l], lambda_q1[l], lambda_k1[l],
                   lambda_q2[l], lambda_k2[l], subln_g[l], w_attn_proj[l], conv_w[l], w_conv_proj[l],
                   w_out[l], norm_ffn_g[l], peer_w_query[l], peer_sub_keys[l], peer_u[l], peer_v[l],
                   lam_init)
    return x
```

```python
import functools
import math

import numpy as np
import jax
import jax.numpy as jnp
from jax import lax
from jax.experimental import pallas as pl
from jax.experimental.pallas import tpu as pltpu

F32 = jnp.float32
BF16 = jnp.bfloat16
I32 = jnp.int32

EPS = 1e-6
HEAD_DIM = 64
V_HEAD_DIM = 2 * HEAD_DIM
CONV_K = 3
PEER_HEADS = 8
N_KEYS = 128
PEER_TOPK = 16
D_HALF = 128
LANES = 128
SUBLANES = 8
MXU_COL = 256
NEG = -1e30
VMEM_LIMIT = 56 * 1024 * 1024

_NT = (((1,), (1,)), ((), ()))


def _cparams(n_axes, vmem=None):
    return pltpu.CompilerParams(
        dimension_semantics=("arbitrary",) * n_axes,
        vmem_limit_bytes=vmem)


def _const_spec(shape):
    nd = len(shape)
    return pl.BlockSpec(shape, lambda *_: (0,) * nd, pipeline_mode=pl.Buffered(1))


def _rmsnorm_kernel(x_ref, g_ref, o_ref):
    x = x_ref[...]
    ms = jnp.mean(x * x, axis=-1, keepdims=True)
    o_ref[...] = ((x * lax.rsqrt(ms + EPS)) * g_ref[...]).astype(o_ref.dtype)


def _rmsnorm(x2d, g, tm):
    n, d = x2d.shape
    return pl.pallas_call(
        _rmsnorm_kernel,
        grid=(n // tm,),
        in_specs=[pl.BlockSpec((tm, d), lambda i: (i, 0)), _const_spec((1, d))],
        out_specs=pl.BlockSpec((tm, d), lambda i: (i, 0)),
        out_shape=jax.ShapeDtypeStruct((n, d), BF16),
        compiler_params=_cparams(1),
        name="rmsnorm_in",
    )(x2d, g.reshape(1, d))


def _qk_kernel(h_ref, w_ref, g_ref, bd_ref, o_ref):
    p = jnp.dot(h_ref[...], w_ref[...], preferred_element_type=F32)
    sq = p * p
    hi = sq.astype(BF16)
    lo = (sq - hi.astype(F32)).astype(BF16)
    bd = bd_ref[...]
    parts = []
    for c in range(p.shape[1] // MXU_COL):
        sl = slice(c * MXU_COL, (c + 1) * MXU_COL)
        parts.append(jnp.dot(hi[:, sl], bd, preferred_element_type=F32)
                     + jnp.dot(lo[:, sl], bd, preferred_element_type=F32))
    ss = jnp.concatenate(parts, axis=1)
    y = p * lax.rsqrt(ss * (1.0 / HEAD_DIM) + EPS)
    o_ref[...] = (y * g_ref[...]).astype(o_ref.dtype)


def _qk_proj(h2d, w_qk, gains, tm):
    n, d = h2d.shape
    width = w_qk.shape[2]
    grp = np.arange(MXU_COL) // HEAD_DIM
    bd = jnp.asarray((grp[:, None] == grp[None, :]).astype(np.float32), BF16)
    return pl.pallas_call(
        _qk_kernel,
        grid=(2, n // tm),
        in_specs=[
            pl.BlockSpec((tm, d), lambda g, i: (i, 0)),
            pl.BlockSpec((None, d, width), lambda g, i: (g, 0, 0)),
            pl.BlockSpec((None, 1, width), lambda g, i: (g, 0, 0)),
            _const_spec((MXU_COL, MXU_COL)),
        ],
        out_specs=pl.BlockSpec((None, tm, width), lambda g, i: (g, i, 0)),
        out_shape=jax.ShapeDtypeStruct((2, n, width), BF16),
        compiler_params=_cparams(2, VMEM_LIMIT),
        name="qk_proj",
    )(h2d, w_qk, gains, bd)


def _vt_kernel(h_ref, wt_ref, o_ref):
    o_ref[...] = lax.dot_general(wt_ref[...], h_ref[...], _NT,
                                 preferred_element_type=F32).astype(o_ref.dtype)


def _vt_proj(h3d, w_vt, tm):
    b, s, d = h3d.shape
    dout = w_vt.shape[0]
    return pl.pallas_call(
        _vt_kernel,
        grid=(b, s // tm),
        in_specs=[pl.BlockSpec((None, tm, d), lambda i, j: (i, j, 0)),
                  _const_spec((dout, d))],
        out_specs=pl.BlockSpec((None, dout, tm), lambda i, j: (i, 0, j)),
        out_shape=jax.ShapeDtypeStruct((b, dout, s), BF16),
        compiler_params=_cparams(2, VMEM_LIMIT),
        name="vt_proj",
    )(h3d, w_vt)


def _gate_kernel(h_ref, w_ref, o_ref):
    p = jnp.dot(h_ref[...], w_ref[...], preferred_element_type=F32)
    o_ref[...] = (1.0 / (1.0 + jnp.exp(-p))).astype(o_ref.dtype)


def _gate_proj(h2d, w_g, tm):
    n, d = h2d.shape
    width = w_g.shape[2]
    return pl.pallas_call(
        _gate_kernel,
        grid=(2, n // tm),
        in_specs=[pl.BlockSpec((tm, d), lambda g, i: (i, 0)),
                  pl.BlockSpec((None, d, width), lambda g, i: (g, 0, 0))],
        out_specs=pl.BlockSpec((None, tm, width), lambda g, i: (g, i, 0)),
        out_shape=jax.ShapeDtypeStruct((2, n, width), BF16),
        compiler_params=_cparams(2, VMEM_LIMIT),
        name="gate_proj",
    )(h2d, w_g)


def _conv_kernel(h_ref, w_ref, cw_ref, o_ref, carry_ref):
    @pl.when(pl.program_id(1) == 0)
    def _():
        carry_ref[...] = jnp.zeros_like(carry_ref)

    width = o_ref.shape[1]
    tm = o_ref.shape[0]
    p = jnp.dot(h_ref[...], w_ref[...], preferred_element_type=F32)
    y = p[:, width:2 * width] * p[:, 2 * width:]
    rows = lax.broadcasted_iota(I32, y.shape, 0)
    prev2 = carry_ref[SUBLANES - 2:SUBLANES - 1, :]
    prev1 = carry_ref[SUBLANES - 1:SUBLANES, :]
    y1 = jnp.where(rows == 0, prev1, pltpu.roll(y, 1, 0))
    y2 = jnp.where(rows == 0, prev2, jnp.where(rows == 1, prev1, pltpu.roll(y, 2, 0)))
    cw = cw_ref[...]
    z = cw[0:1, :] * y2 + cw[1:2, :] * y1 + cw[2:3, :] * y
    o_ref[...] = (p[:, :width] * z).astype(o_ref.dtype)
    carry_ref[...] = y[tm - SUBLANES:, :]


def _conv_branch(h3d, w_c, conv_w, tm):
    b, s, d = h3d.shape
    width = conv_w.shape[1]
    return pl.pallas_call(
        _conv_kernel,
        grid=(b, s // tm),
        in_specs=[pl.BlockSpec((None, tm, d), lambda i, j: (i, j, 0)),
                  _const_spec((d, 3 * width)),
                  _const_spec((CONV_K, width))],
        out_specs=pl.BlockSpec((None, tm, width), lambda i, j: (i, j, 0)),
        out_shape=jax.ShapeDtypeStruct((b, s, width), BF16),
        scratch_shapes=[pltpu.VMEM((SUBLANES, width), F32)],
        compiler_params=_cparams(2, VMEM_LIMIT),
        name="conv_branch",
    )(h3d, w_c, conv_w)


def _attn_kernel(slopes_ref, q_ref, k_ref, vt_ref, lq1_ref, lk1_ref, lq2_ref, lk2_ref, sg_ref,
                 o_ref, kaug_ref, qaug_ref, m_ref, l_ref, acc_ref, s0_ref, *, tq, cw, lam_init):
    h = pl.program_id(1)
    qi = pl.program_id(2)
    s_len = k_ref.shape[0]
    slope = slopes_ref[h]

    @pl.when(qi == 0)
    def _():
        kpos = lax.broadcasted_iota(I32, (s_len, LANES), 0).astype(F32) * slope
        lane = lax.broadcasted_iota(I32, (s_len, LANES), 1)
        b_hi = kpos.astype(BF16).astype(F32)
        b_lo = kpos - b_hi
        extra = jnp.where(lane == 0, b_hi, jnp.where(lane == 1, b_lo, jnp.where(lane < 4, 1.0, 0.0)))
        kaug_ref[:, :LANES] = k_ref[...]
        kaug_ref[:, LANES:] = extra.astype(BF16)

    q = q_ref[...].astype(F32)
    lane = lax.broadcasted_iota(I32, (tq, LANES), 1)
    c_full = -((jnp.zeros((tq, LANES), I32) + qi * tq).astype(F32) * slope)
    c_hi = c_full.astype(BF16).astype(F32)
    c_lo = c_full - c_hi
    qextra = jnp.where(lane < 2, 1.0, jnp.where(lane == 2, c_hi, jnp.where(lane == 3, c_lo, 0.0)))
    qextra = qextra.astype(BF16)
    qaug_ref[:tq, :LANES] = jnp.where(lane < HEAD_DIM, q, 0.0).astype(BF16)
    qaug_ref[tq:, :LANES] = jnp.where(lane >= HEAD_DIM, q, 0.0).astype(BF16)
    qaug_ref[:tq, LANES:] = qextra
    qaug_ref[tq:, LANES:] = qextra

    m_ref[...] = jnp.full(m_ref.shape, NEG, F32)
    l_ref[...] = jnp.zeros(l_ref.shape, F32)
    acc_ref[...] = jnp.zeros(acc_ref.shape, F32)

    def scores(kt, c, masked):
        lo = c * cw
        q_lo = lo % tq
        kn = q_lo + cw if masked else tq
        k0 = pl.multiple_of(kt * tq, tq)
        s = lax.dot_general(kaug_ref[pl.ds(k0, kn), :], qaug_ref[lo:lo + cw, :], _NT,
                            preferred_element_type=F32)
        if masked:
            kk = lax.broadcasted_iota(I32, s.shape, 0)
            qq = lax.broadcasted_iota(I32, s.shape, 1) + q_lo
            s = jnp.where(kk <= qq, s, NEG)
        return s

    def update(kt, c, s):
        lo = c * cw
        kn = s.shape[0]
        k0 = pl.multiple_of(kt * tq, tq)
        m_old = m_ref[:, lo:lo + cw]
        m_new = jnp.maximum(m_old, jnp.max(s, axis=0, keepdims=True))
        alpha = jnp.exp(m_old - m_new)
        p = jnp.exp(s - m_new)
        l_ref[:, lo:lo + cw] = alpha * l_ref[:, lo:lo + cw] + jnp.sum(p, axis=0, keepdims=True)
        pv = jnp.dot(vt_ref[:, pl.ds(k0, kn)], p.astype(BF16), preferred_element_type=F32)
        acc_ref[:, lo:lo + cw] = alpha * acc_ref[:, lo:lo + cw] + pv
        m_ref[:, lo:lo + cw] = m_new

    def step(kt, masked):
        n_chunks = 2 * tq // cw
        if masked:
            s_next = s0_ref[:cw, :]
            kk = lax.broadcasted_iota(I32, s_next.shape, 0)
            qq = lax.broadcasted_iota(I32, s_next.shape, 1)
            s_next = jnp.where(kk <= qq, s_next, NEG)
        else:
            s_next = s0_ref[...]
        for c in range(n_chunks):
            s_cur = s_next
            if c + 1 < n_chunks:
                s_next = scores(kt, c + 1, masked)
            update(kt, c, s_cur)
        if not masked:
            s0_ref[...] = scores(kt + 1, 0, False)

    def body(kt, carry):
        step(kt, False)
        return carry

    s0_ref[...] = scores(0, 0, False)
    lax.fori_loop(0, qi, body, 0)
    step(qi, True)

    lam = (jnp.exp(jnp.sum(lq1_ref[...] * lk1_ref[...], axis=1, keepdims=True))
           - jnp.exp(jnp.sum(lq2_ref[...] * lk2_ref[...], axis=1, keepdims=True))
           + lam_init)
    o = acc_ref[...] / l_ref[...]
    a = o[:, :tq] - lam * o[:, tq:]
    ms = jnp.mean(a * a, axis=0, keepdims=True)
    a = (a * lax.rsqrt(ms + EPS)) * sg_ref[...]
    a = a * (1.0 - lam_init)
    o_ref[...] = a.T.astype(o_ref.dtype)


def _diff_attention(q3d, k3d, vt3d, slopes, lq1, lk1, lq2, lk2, subln_g, lam_init, tq, cw):
    b, s, width = q3d.shape
    n_heads = width // V_HEAD_DIM
    lam_spec = pl.BlockSpec((1, HEAD_DIM), lambda i, h, j, *_: (0, 0))
    grid_spec = pltpu.PrefetchScalarGridSpec(
        num_scalar_prefetch=1,
        grid=(b, n_heads, s // tq),
        in_specs=[
            pl.BlockSpec((None, tq, V_HEAD_DIM), lambda i, h, j, *_: (i, j, h)),
            pl.BlockSpec((None, s, V_HEAD_DIM), lambda i, h, j, *_: (i, 0, h)),
            pl.BlockSpec((None, V_HEAD_DIM, s), lambda i, h, j, *_: (i, h, 0)),
            lam_spec, lam_spec, lam_spec, lam_spec,
            pl.BlockSpec((V_HEAD_DIM, 1), lambda i, h, j, *_: (0, 0)),
        ],
        out_specs=pl.BlockSpec((None, tq, V_HEAD_DIM), lambda i, h, j, *_: (i, j, h)),
        scratch_shapes=[
            pltpu.VMEM((s, 2 * LANES), BF16),
            pltpu.VMEM((2 * tq, 2 * LANES), BF16),
            pltpu.VMEM((1, 2 * tq), F32),
            pltpu.VMEM((1, 2 * tq), F32),
            pltpu.VMEM((V_HEAD_DIM, 2 * tq), F32),
            pltpu.VMEM((tq, cw), F32),
        ],
    )
    return pl.pallas_call(
        functools.partial(_attn_kernel, tq=tq, cw=cw, lam_init=lam_init),
        grid_spec=grid_spec,
        out_shape=jax.ShapeDtypeStruct((b, s, width), BF16),
        compiler_params=_cparams(3, VMEM_LIMIT),
        name="diff_attention",
    )(slopes, q3d, k3d, vt3d, lq1.reshape(1, -1), lk1.reshape(1, -1), lq2.reshape(1, -1),
      lk2.reshape(1, -1), subln_g.reshape(-1, 1))


def _mix_kernel(x_ref, attn_ref, conv_ref, sga_ref, sgc_ref, wa_ref, wc_ref, wo_ref, gf_ref, wq_ref,
                x1_ref, h2_ref, qp_ref):
    a = jnp.dot(attn_ref[...], wa_ref[...], preferred_element_type=F32)
    c = jnp.dot(conv_ref[...], wc_ref[...], preferred_element_type=F32)
    mixed = sga_ref[...].astype(F32) * a + sgc_ref[...].astype(F32) * c
    x1 = x_ref[...] + jnp.dot(mixed.astype(BF16), wo_ref[...], preferred_element_type=F32)
    ms = jnp.mean(x1 * x1, axis=-1, keepdims=True)
    h2 = (x1 * lax.rsqrt(ms + EPS)) * gf_ref[...]
    for s in range(x1_ref.shape[1]):
        x1_ref[:, s, :] = x1[:, s * LANES:(s + 1) * LANES]
        h2_ref[:, s, :] = h2[:, s * LANES:(s + 1) * LANES]
    qp_ref[...] = jnp.dot(h2.astype(BF16), wq_ref[...], preferred_element_type=F32).astype(qp_ref.dtype)


def _mix(x2d, attn2d, conv2d, sg, wa, wc, wo, gf, wq, tm):
    n, d = x2d.shape
    qw = wq.shape[1]
    row = lambda width: pl.BlockSpec((tm, width), lambda i: (i, 0))
    tiled = pl.BlockSpec((tm, d // LANES, LANES), lambda i: (i, 0, 0))
    return pl.pallas_call(
        _mix_kernel,
        grid=(n // tm,),
        in_specs=[row(d), row(d), row(d),
                  pl.BlockSpec((None, tm, d), lambda i: (0, i, 0)),
                  pl.BlockSpec((None, tm, d), lambda i: (1, i, 0)),
                  _const_spec((d, d)), _const_spec((d, d)), _const_spec((d, d)),
                  _const_spec((1, d)), _const_spec((d, qw))],
        out_specs=[tiled, tiled, row(qw)],
        out_shape=[jax.ShapeDtypeStruct((n, d // LANES, LANES), F32),
                   jax.ShapeDtypeStruct((n, d // LANES, LANES), F32),
                   jax.ShapeDtypeStruct((n, qw), BF16)],
        compiler_params=_cparams(1, VMEM_LIMIT),
        name="mix_residual",
    )(x2d, attn2d, conv2d, sg, sg, wa, wc, wo, gf.reshape(1, d), wq)


def _top_rounds(c, rowid, payload, n_rounds):
    vals, rids, pays = [], [], []
    for _ in range(n_rounds):
        m = jnp.max(c, axis=0, keepdims=True)
        rid = jnp.min(jnp.where(c == m, rowid, 1e9), axis=0, keepdims=True)
        hit = rowid == rid
        vals.append(m)
        rids.append(rid)
        if payload is not None:
            pays.append(jnp.sum(jnp.where(hit, payload, 0.0), axis=0, keepdims=True))
        c = jnp.where(hit, NEG, c)
    return vals, rids, pays


def _stack_rows(rows):
    n, tt = len(rows), rows[0].shape[1]
    rid = lax.broadcasted_iota(I32, (n, tt), 0)
    out = jnp.zeros((n, tt), rows[0].dtype)
    for r, row in enumerate(rows):
        out = jnp.where(rid == r, row, out)
    return out


def _candidates(sv, si):
    assert PEER_TOPK == 2 * SUBLANES
    tt = sv[0].shape[1]
    sub = lax.broadcasted_iota(I32, (SUBLANES, tt), 0)
    subf = sub.astype(F32)
    vals, ids, exps = [], [], []
    for b0 in (0, SUBLANES):
        vals.append(sv[0][0:1, :] + sv[1][b0:b0 + SUBLANES, :])
        ids.append(subf + float(b0))
        exps.append(si[0][0:1, :] * float(N_KEYS) + si[1][b0:b0 + SUBLANES, :])
    for a in range(1, SUBLANES):
        v = sv[0][a:a + 1, :] + sv[1][0:SUBLANES, :]
        vals.append(jnp.where(sub < PEER_TOPK // (a + 1), v, NEG))
        ids.append(subf + float(a * PEER_TOPK))
        exps.append(si[0][a:a + 1, :] * float(N_KEYS) + si[1][0:SUBLANES, :])
    vals.append(sv[0][SUBLANES:, :] + sv[1][0:1, :])
    ids.append((subf + float(SUBLANES)) * float(PEER_TOPK))
    exps.append(si[0][SUBLANES:, :] * float(N_KEYS) + si[1][0:1, :])
    cat = lambda xs: jnp.concatenate(xs, axis=0)
    return cat(vals), cat(ids), cat(exps)


def _topk_kernel(qp_ref, sk_ref, eidx_ref, g_ref, *, row_words):
    tt = qp_ref.shape[0]
    key_id = lax.broadcasted_iota(I32, (N_KEYS, tt), 0).astype(F32)
    e_rows, g_rows = [], []
    for h in range(PEER_HEADS):
        sv, si = [], []
        for p in range(2):
            hp = 2 * h + p
            q = qp_ref[:, hp * D_HALF:(hp + 1) * D_HALF]
            s = lax.dot_general(sk_ref[hp], q, _NT, preferred_element_type=F32)
            vals, rids, _ = _top_rounds(s, key_id, None, PEER_TOPK)
            sv.append(_stack_rows(vals))
            si.append(_stack_rows(rids))
        cand, cand_id, cidx = _candidates(sv, si)
        tv, _, te = _top_rounds(cand, cand_id, cidx, PEER_TOPK)
        tv = _stack_rows(tv)
        ex = jnp.exp(tv - tv[0:1, :])
        g_rows.append(ex / jnp.sum(ex, axis=0, keepdims=True))
        e_rows.append(_stack_rows(te))
    eidx_ref[...] = (jnp.concatenate(e_rows, axis=0).T * float(row_words)).astype(I32)
    g_ref[...] = jnp.concatenate(g_rows, axis=0).T


def _peer_topk(qp, sub_keys, row_words, tt):
    n, qw = qp.shape
    nk = PEER_HEADS * PEER_TOPK
    return pl.pallas_call(
        functools.partial(_topk_kernel, row_words=row_words),
        grid=(n // tt,),
        in_specs=[pl.BlockSpec((tt, qw), lambda i: (i, 0)),
                  _const_spec(sub_keys.shape)],
        out_specs=[pl.BlockSpec((tt, nk), lambda i: (i, 0)),
                   pl.BlockSpec((tt, nk), lambda i: (i, 0))],
        out_shape=[jax.ShapeDtypeStruct((n, nk), I32),
                   jax.ShapeDtypeStruct((n, nk), F32)],
        compiler_params=_cparams(1, VMEM_LIMIT),
        name="peer_topk",
    )(qp, sub_keys)


def _pack_kernel(t_ref, o_ref, *, words):
    rows = t_ref.shape[0]
    for j in range(words):
        lo = t_ref[:, (2 * j) * LANES:(2 * j + 1) * LANES].astype(BF16).astype(F32)
        hi = t_ref[:, (2 * j + 1) * LANES:(2 * j + 2) * LANES].astype(BF16).astype(F32)
        w = (pltpu.bitcast(lo, jnp.uint32) >> 16) | (pltpu.bitcast(hi, jnp.uint32) & jnp.uint32(0xFFFF0000))
        o_ref[pl.ds(j, rows, stride=words), :] = pltpu.bitcast(w, I32)


def _pack_table(t, rows=512):
    e, d = t.shape
    words = d // (2 * LANES)
    rows = _tile(e, rows)
    return pl.pallas_call(
        functools.partial(_pack_kernel, words=words),
        grid=(e // rows,),
        in_specs=[pl.BlockSpec((rows, d), lambda i: (i, 0))],
        out_specs=pl.BlockSpec((rows * words, LANES), lambda i: (i, 0)),
        out_shape=jax.ShapeDtypeStruct((e * words, LANES), I32),
        compiler_params=_cparams(1),
        name="pack_table",
    )(t)


def _gather_rows(idx_row, tab_ref, gbuf_ref, n_sel, words):
    for k in range(n_sel):
        r = pl.multiple_of(idx_row[k], words)
        gbuf_ref[pl.ds(k * words, words), :] = tab_ref[pl.ds(r, words), :]


def _diag_mask(rows, ds, n_sel):
    col = lax.broadcasted_iota(I32, (rows, ds * n_sel), 1)
    row = lax.broadcasted_iota(I32, (rows, ds * n_sel), 0)
    return (col % ds) == (row % ds)


def _gelu(a):
    return 0.5 * a * (1.0 + lax.erf(a * (1.0 / math.sqrt(2.0))))


def _split_bf16(x):
    hi = x.astype(BF16)
    return hi, (x - hi.astype(F32)).astype(BF16)


def _peer_u_kernel(eidx_ref, h_ref, g_ref, u_ref, sel_ref, w_ref, gbuf_ref, dbuf_ref, *, unroll):
    tt, ds, _ = h_ref.shape
    n_sel = g_ref.shape[1]
    words = ds // 2
    mask = _diag_mask(ds, ds, n_sel)

    def group(i, carry):
        for j in range(unroll):
            _gather_rows(eidx_ref.at[i * unroll + j], u_ref, gbuf_ref.at[j], n_sel, words)
        for j in range(unroll):
            t = i * unroll + j
            g2 = pltpu.bitcast(gbuf_ref[j], BF16)
            hs = h_ref[t].astype(BF16)
            r = lax.dot_general(hs, g2, _NT, preferred_element_type=F32)
            dbuf_ref[pl.ds(t, 1), :] = jnp.sum(jnp.where(mask, r, 0.0), axis=0, keepdims=True)
        return carry

    lax.fori_loop(0, tt // unroll, group, 0)
    sel = sel_ref[...]
    d_hi, d_lo = _split_bf16(dbuf_ref[...])
    a = (jnp.dot(d_hi, sel, preferred_element_type=F32)
         + jnp.dot(d_lo, sel, preferred_element_type=F32))
    w_ref[...] = g_ref[...] * _gelu(a)


def _peer_v_kernel(eidx_ref, w_ref, x1_ref, v_ref, selt_ref, o_ref, gbuf_ref, wrep_ref, *, unroll):
    tt, ds, _ = x1_ref.shape
    n_sel = w_ref.shape[1]
    words = ds // 2
    mask = _diag_mask(ds, ds, n_sel)
    selt = selt_ref[...]
    w_hi, w_lo = _split_bf16(w_ref[...])
    wrep_ref[...] = (jnp.dot(w_hi, selt, preferred_element_type=F32)
                     + jnp.dot(w_lo, selt, preferred_element_type=F32))

    def group(i, carry):
        for j in range(unroll):
            _gather_rows(eidx_ref.at[i * unroll + j], v_ref, gbuf_ref.at[j], n_sel, words)
        for j in range(unroll):
            t = i * unroll + j
            g2 = pltpu.bitcast(gbuf_ref[j], BF16)
            wr = jnp.broadcast_to(wrep_ref[pl.ds(t, 1), :], mask.shape)
            wm_hi, wm_lo = _split_bf16(jnp.where(mask, wr, 0.0))
            o = (jnp.dot(wm_hi, g2, preferred_element_type=F32)
                 + jnp.dot(wm_lo, g2, preferred_element_type=F32))
            o_ref[t] = x1_ref[t] + o
        return carry

    lax.fori_loop(0, tt // unroll, group, 0)


def _sel_matrix(ds, n_sel):
    grp = np.arange(ds * n_sel) // ds
    return (grp[:, None] == np.arange(n_sel)[None, :]).astype(np.float32)


PEER_UNROLL = 16


def _peer_u(eidx, h3, g, u_packed, tt):
    n, ds, _ = h3.shape
    n_sel = g.shape[1]
    words = ds // 2
    sel = jnp.asarray(_sel_matrix(ds, n_sel), BF16)
    return pl.pallas_call(
        functools.partial(_peer_u_kernel, unroll=PEER_UNROLL),
        grid=(n // tt,),
        in_specs=[pl.BlockSpec((tt, n_sel), lambda i: (i, 0), memory_space=pltpu.SMEM),
                  pl.BlockSpec((tt, ds, LANES), lambda i: (i, 0, 0)),
                  pl.BlockSpec((tt, n_sel), lambda i: (i, 0)),
                  _const_spec(u_packed.shape),
                  _const_spec(sel.shape)],
        out_specs=pl.BlockSpec((tt, n_sel), lambda i: (i, 0)),
        out_shape=jax.ShapeDtypeStruct((n, n_sel), F32),
        scratch_shapes=[pltpu.VMEM((PEER_UNROLL, n_sel * words, LANES), I32),
                        pltpu.VMEM((tt, n_sel * ds), F32)],
        compiler_params=_cparams(1, VMEM_LIMIT),
        name="peer_u",
    )(eidx, h3, g, u_packed, sel)


def _peer_v(eidx, w, x1_3, v_packed, tt):
    n, ds, _ = x1_3.shape
    n_sel = w.shape[1]
    words = ds // 2
    selt = jnp.asarray(_sel_matrix(ds, n_sel).T, BF16)
    return pl.pallas_call(
        functools.partial(_peer_v_kernel, unroll=PEER_UNROLL),
        grid=(n // tt,),
        in_specs=[pl.BlockSpec((tt, n_sel), lambda i: (i, 0), memory_space=pltpu.SMEM),
                  pl.BlockSpec((tt, n_sel), lambda i: (i, 0)),
                  pl.BlockSpec((tt, ds, LANES), lambda i: (i, 0, 0)),
                  _const_spec(v_packed.shape),
                  _const_spec(selt.shape)],
        out_specs=pl.BlockSpec((tt, ds, LANES), lambda i: (i, 0, 0)),
        out_shape=jax.ShapeDtypeStruct((n, ds, LANES), F32),
        scratch_shapes=[pltpu.VMEM((PEER_UNROLL, n_sel * words, LANES), I32),
                        pltpu.VMEM((tt, n_sel * ds), F32)],
        compiler_params=_cparams(1, VMEM_LIMIT),
        name="peer_v",
    )(eidx, w, x1_3, v_packed, selt)


def _tile(n, pref):
    t = min(n, pref)
    while n % t:
        t //= 2
    return t


def _layer(x, norm_mix_g, w_in, q_norm_g, k_norm_g, lq1, lk1, lq2, lk2, subln_g, w_attn_proj,
           conv_w, w_conv_proj, w_out, norm_ffn_g, w_query, sub_keys, peer_u, peer_v, lam_init):
    b, s, d = x.shape
    n = b * s
    n_heads = d // V_HEAD_DIM
    qk_w = n_heads * 2 * HEAD_DIM
    x2d = x.reshape(n, d)

    wb = w_in.astype(BF16)
    o = 0
    w_qk = jnp.stack([wb[:, 0:qk_w], wb[:, qk_w:2 * qk_w]])
    o = 2 * qk_w
    w_vt = wb[:, o:o + d].T
    o += d
    w_c = wb[:, o:o + 3 * d]
    o += 3 * d
    w_g = jnp.stack([wb[:, o:o + d], wb[:, o + d:o + 2 * d]])
    reps = qk_w // HEAD_DIM
    gains = jnp.stack([jnp.tile(q_norm_g.astype(F32), reps) * (HEAD_DIM ** -0.5),
                       jnp.tile(k_norm_g.astype(F32), reps)]).reshape(2, 1, qk_w)
    slopes = jnp.asarray(np.array([2.0 ** (-8.0 * (i + 1) / n_heads) for i in range(n_heads)],
                                  dtype=np.float32))

    h = _rmsnorm(x2d, norm_mix_g.astype(F32), _tile(n, 1024))
    h3d = h.reshape(b, s, d)
    qk = _qk_proj(h, w_qk, gains, _tile(n, 1024))
    vt = _vt_proj(h3d, w_vt, _tile(s, 512))
    sg = _gate_proj(h, w_g, _tile(n, 1024))
    convb = _conv_branch(h3d, w_c, conv_w.astype(F32), _tile(s, 512))
    attn = _diff_attention(qk[0].reshape(b, s, qk_w), qk[1].reshape(b, s, qk_w), vt, slopes,
                           lq1.astype(F32), lk1.astype(F32), lq2.astype(F32), lk2.astype(F32),
                           subln_g.astype(F32), lam_init, _tile(s, 512), MXU_COL)
    x1, h2, qp = _mix(x2d, attn.reshape(n, d), convb.reshape(n, d), sg,
                      w_attn_proj.astype(BF16), w_conv_proj.astype(BF16), w_out.astype(BF16),
                      norm_ffn_g.astype(F32), w_query.astype(BF16), _tile(n, 512))

    skb = sub_keys.astype(BF16).reshape(PEER_HEADS * 2, N_KEYS, D_HALF)
    ds = d // LANES
    eidx, g = _peer_topk(qp, skb, ds // 2, _tile(n, 256))
    tt = _tile(n, 128)
    w = _peer_u(eidx, h2, g, _pack_table(peer_u), tt)
    out = _peer_v(eidx, w, x1, _pack_table(peer_v), tt)
    return out.reshape(b, s, d)


def kernel(x, norm_mix_g, w_in, q_norm_g, k_norm_g, lambda_q1, lambda_k1, lambda_q2, lambda_k2, subln_g, w_attn_proj, conv_w, w_conv_proj, w_out, norm_ffn_g, peer_w_query, peer_sub_keys, peer_u, peer_v):
    depth = w_in.shape[0]
    for l in range(depth):
        lam_init = 0.8 - 0.6 * math.exp(-0.3 * l)
        x = _layer(x, norm_mix_g[l], w_in[l], q_norm_g[l], k_norm_g[l], lambda_q1[l], lambda_k1[l],
                   lambda_q2[l], lambda_k2[l], subln_g[l], w_attn_proj[l], conv_w[l], w_conv_proj[l],
                   w_out[l], norm_ffn_g[l], peer_w_query[l], peer_sub_keys[l], peer_u[l], peer_v[l],
                   lam_init)
    return x
```

```python
import functools
import math

import numpy as np
import jax
import jax.numpy as jnp
from jax import lax
from jax.experimental import pallas as pl
from jax.experimental.pallas import tpu as pltpu

F32 = jnp.float32
BF16 = jnp.bfloat16
I32 = jnp.int32

EPS = 1e-6
HEAD_DIM = 64
V_HEAD_DIM = 2 * HEAD_DIM
CONV_K = 3
PEER_HEADS = 8
N_KEYS = 128
PEER_TOPK = 16
D_HALF = 128
LANES = 128
SUBLANES = 8
MXU_COL = 256
NEG = -1e30
VMEM_LIMIT = 56 * 1024 * 1024

_NT = (((1,), (1,)), ((), ()))


def _cparams(n_axes, vmem=None):
    return pltpu.CompilerParams(
        dimension_semantics=("arbitrary",) * n_axes,
        vmem_limit_bytes=vmem)


def _const_spec(shape):
    nd = len(shape)
    return pl.BlockSpec(shape, lambda *_: (0,) * nd, pipeline_mode=pl.Buffered(1))


def _rmsnorm_kernel(x_ref, g_ref, o_ref):
    x = x_ref[...]
    ms = jnp.mean(x * x, axis=-1, keepdims=True)
    o_ref[...] = ((x * lax.rsqrt(ms + EPS)) * g_ref[...]).astype(o_ref.dtype)


def _rmsnorm(x2d, g, tm):
    n, d = x2d.shape
    return pl.pallas_call(
        _rmsnorm_kernel,
        grid=(n // tm,),
        in_specs=[pl.BlockSpec((tm, d), lambda i: (i, 0)), _const_spec((1, d))],
        out_specs=pl.BlockSpec((tm, d), lambda i: (i, 0)),
        out_shape=jax.ShapeDtypeStruct((n, d), BF16),
        compiler_params=_cparams(1),
        name="rmsnorm_in",
    )(x2d, g.reshape(1, d))


def _qk_kernel(h_ref, w_ref, g_ref, bd_ref, o_ref):
    p = jnp.dot(h_ref[...], w_ref[...], preferred_element_type=F32)
    sq = p * p
    hi = sq.astype(BF16)
    lo = (sq - hi.astype(F32)).astype(BF16)
    bd = bd_ref[...]
    parts = []
    for c in range(p.shape[1] // MXU_COL):
        sl = slice(c * MXU_COL, (c + 1) * MXU_COL)
        parts.append(jnp.dot(hi[:, sl], bd, preferred_element_type=F32)
                     + jnp.dot(lo[:, sl], bd, preferred_element_type=F32))
    ss = jnp.concatenate(parts, axis=1)
    y = p * lax.rsqrt(ss * (1.0 / HEAD_DIM) + EPS)
    o_ref[...] = (y * g_ref[...]).astype(o_ref.dtype)


def _qk_proj(h2d, w_qk, gains, tm):
    n, d = h2d.shape
    width = w_qk.shape[2]
    grp = np.arange(MXU_COL) // HEAD_DIM
    bd = jnp.asarray((grp[:, None] == grp[None, :]).astype(np.float32), BF16)
    return pl.pallas_call(
        _qk_kernel,
        grid=(2, n // tm),
        in_specs=[
            pl.BlockSpec((tm, d), lambda g, i: (i, 0)),
            pl.BlockSpec((None, d, width), lambda g, i: (g, 0, 0)),
            pl.BlockSpec((None, 1, width), lambda g, i: (g, 0, 0)),
            _const_spec((MXU_COL, MXU_COL)),
        ],
        out_specs=pl.BlockSpec((None, tm, width), lambda g, i: (g, i, 0)),
        out_shape=jax.ShapeDtypeStruct((2, n, width), BF16),
        compiler_params=_cparams(2, VMEM_LIMIT),
        name="qk_proj",
    )(h2d, w_qk, gains, bd)


def _vt_kernel(h_ref, wt_ref, o_ref):
    o_ref[...] = lax.dot_general(wt_ref[...], h_ref[...], _NT,
                                 preferred_element_type=F32).astype(o_ref.dtype)


def _vt_proj(h3d, w_vt, tm):
    b, s, d = h3d.shape
    dout = w_vt.shape[0]
    return pl.pallas_call(
        _vt_kernel,
        grid=(b, s // tm),
        in_specs=[pl.BlockSpec((None, tm, d), lambda i, j: (i, j, 0)),
                  _const_spec((dout, d))],
        out_specs=pl.BlockSpec((None, dout, tm), lambda i, j: (i, 0, j)),
        out_shape=jax.ShapeDtypeStruct((b, dout, s), BF16),
        compiler_params=_cparams(2, VMEM_LIMIT),
        name="vt_proj",
    )(h3d, w_vt)


def _gate_kernel(h_ref, w_ref, o_ref):
    p = jnp.dot(h_ref[...], w_ref[...], preferred_element_type=F32)
    o_ref[...] = (1.0 / (1.0 + jnp.exp(-p))).astype(o_ref.dtype)


def _gate_proj(h2d, w_g, tm):
    n, d = h2d.shape
    width = w_g.shape[2]
    return pl.pallas_call(
        _gate_kernel,
        grid=(2, n // tm),
        in_specs=[pl.BlockSpec((tm, d), lambda g, i: (i, 0)),
                  pl.BlockSpec((None, d, width), lambda g, i: (g, 0, 0))],
        out_specs=pl.BlockSpec((None, tm, width), lambda g, i: (g, i, 0)),
        out_shape=jax.ShapeDtypeStruct((2, n, width), BF16),
        compiler_params=_cparams(2, VMEM_LIMIT),
        name="gate_proj",
    )(h2d, w_g)


def _conv_kernel(h_ref, w_ref, cw_ref, o_ref, carry_ref):
    @pl.when(pl.program_id(1) == 0)
    def _():
        carry_ref[...] = jnp.zeros_like(carry_ref)

    width = o_ref.shape[1]
    tm = o_ref.shape[0]
    p = jnp.dot(h_ref[...], w_ref[...], preferred_element_type=F32)
    y = p[:, width:2 * width] * p[:, 2 * width:]
    rows = lax.broadcasted_iota(I32, y.shape, 0)
    prev2 = carry_ref[SUBLANES - 2:SUBLANES - 1, :]
    prev1 = carry_ref[SUBLANES - 1:SUBLANES, :]
    y1 = jnp.where(rows == 0, prev1, pltpu.roll(y, 1, 0))
    y2 = jnp.where(rows == 0, prev2, jnp.where(rows == 1, prev1, pltpu.roll(y, 2, 0)))
    cw = cw_ref[...]
    z = cw[0:1, :] * y2 + cw[1:2, :] * y1 + cw[2:3, :] * y
    o_ref[...] = (p[:, :width] * z).astype(o_ref.dtype)
    carry_ref[...] = y[tm - SUBLANES:, :]


def _conv_branch(h3d, w_c, conv_w, tm):
    b, s, d = h3d.shape
    width = conv_w.shape[1]
    return pl.pallas_call(
        _conv_kernel,
        grid=(b, s // tm),
        in_specs=[pl.BlockSpec((None, tm, d), lambda i, j: (i, j, 0)),
                  _const_spec((d, 3 * width)),
                  _const_spec((CONV_K, width))],
        out_specs=pl.BlockSpec((None, tm, width), lambda i, j: (i, j, 0)),
        out_shape=jax.ShapeDtypeStruct((b, s, width), BF16),
        scratch_shapes=[pltpu.VMEM((SUBLANES, width), F32)],
        compiler_params=_cparams(2, VMEM_LIMIT),
        name="conv_branch",
    )(h3d, w_c, conv_w)


def _attn_kernel(slopes_ref, q_ref, k_ref, vt_ref, lq1_ref, lk1_ref, lq2_ref, lk2_ref, sg_ref,
                 o_ref, kaug_ref, qaug_ref, m_ref, l_ref, acc_ref, s0_ref, *, tq, cw, lam_init):
    h = pl.program_id(1)
    qi = pl.program_id(2)
    s_len = k_ref.shape[0]
    slope = slopes_ref[h]

    @pl.when(qi == 0)
    def _():
        kpos = lax.broadcasted_iota(I32, (s_len, LANES), 0).astype(F32) * slope
        lane = lax.broadcasted_iota(I32, (s_len, LANES), 1)
        b_hi = kpos.astype(BF16).astype(F32)
        b_lo = kpos - b_hi
        extra = jnp.where(lane == 0, b_hi, jnp.where(lane == 1, b_lo, jnp.where(lane < 4, 1.0, 0.0)))
        kaug_ref[:, :LANES] = k_ref[...]
        kaug_ref[:, LANES:] = extra.astype(BF16)

    q = q_ref[...].astype(F32)
    lane = lax.broadcasted_iota(I32, (tq, LANES), 1)
    c_full = -((jnp.zeros((tq, LANES), I32) + qi * tq).astype(F32) * slope)
    c_hi = c_full.astype(BF16).astype(F32)
    c_lo = c_full - c_hi
    qextra = jnp.where(lane < 2, 1.0, jnp.where(lane == 2, c_hi, jnp.where(lane == 3, c_lo, 0.0)))
    qextra = qextra.astype(BF16)
    qaug_ref[:tq, :LANES] = jnp.where(lane < HEAD_DIM, q, 0.0).astype(BF16)
    qaug_ref[tq:, :LANES] = jnp.where(lane >= HEAD_DIM, q, 0.0).astype(BF16)
    qaug_ref[:tq, LANES:] = qextra
    qaug_ref[tq:, LANES:] = qextra

    m_ref[...] = jnp.full(m_ref.shape, NEG, F32)
    l_ref[...] = jnp.zeros(l_ref.shape, F32)
    acc_ref[...] = jnp.zeros(acc_ref.shape, F32)

    def scores(kt, c, masked):
        lo = c * cw
        q_lo = lo % tq
        kn = q_lo + cw if masked else tq
        k0 = pl.multiple_of(kt * tq, tq)
        s = lax.dot_general(kaug_ref[pl.ds(k0, kn), :], qaug_ref[lo:lo + cw, :], _NT,
                            preferred_element_type=F32)
        if masked:
            kk = lax.broadcasted_iota(I32, s.shape, 0)
            qq = lax.broadcasted_iota(I32, s.shape, 1) + q_lo
            s = jnp.where(kk <= qq, s, NEG)
        return s

    def update(kt, c, s):
        lo = c * cw
        kn = s.shape[0]
        k0 = pl.multiple_of(kt * tq, tq)
        m_old = m_ref[:, lo:lo + cw]
        m_new = jnp.maximum(m_old, jnp.max(s, axis=0, keepdims=True))
        alpha = jnp.exp(m_old - m_new)
        p = jnp.exp(s - m_new)
        l_ref[:, lo:lo + cw] = alpha * l_ref[:, lo:lo + cw] + jnp.sum(p, axis=0, keepdims=True)
        pv = jnp.dot(vt_ref[:, pl.ds(k0, kn)], p.astype(BF16), preferred_element_type=F32)
        acc_ref[:, lo:lo + cw] = alpha * acc_ref[:, lo:lo + cw] + pv
        m_ref[:, lo:lo + cw] = m_new

    def step(kt, masked):
        n_chunks = 2 * tq // cw
        if masked:
            s_next = s0_ref[:cw, :]
            kk = lax.broadcasted_iota(I32, s_next.shape, 0)
            qq = lax.broadcasted_iota(I32, s_next.shape, 1)
            s_next = jnp.where(kk <= qq, s_next, NEG)
        else:
            s_next = s0_ref[...]
        for c in range(n_chunks):
            s_cur = s_next
            if c + 1 < n_chunks:
                s_next = scores(kt, c + 1, masked)
            update(kt, c, s_cur)
        if not masked:
            s0_ref[...] = scores(kt + 1, 0, False)

    def body(kt, carry):
        step(kt, False)
        return carry

    s0_ref[...] = scores(0, 0, False)
    lax.fori_loop(0, qi, body, 0)
    step(qi, True)

    lam = (jnp.exp(jnp.sum(lq1_ref[...] * lk1_ref[...], axis=1, keepdims=True))
           - jnp.exp(jnp.sum(lq2_ref[...] * lk2_ref[...], axis=1, keepdims=True))
           + lam_init)
    o = acc_ref[...] / l_ref[...]
    a = o[:, :tq] - lam * o[:, tq:]
    ms = jnp.mean(a * a, axis=0, keepdims=True)
    a = (a * lax.rsqrt(ms + EPS)) * sg_ref[...]
    a = a * (1.0 - lam_init)
    o_ref[...] = a.T.astype(o_ref.dtype)


def _diff_attention(q3d, k3d, vt3d, slopes, lq1, lk1, lq2, lk2, subln_g, lam_init, tq, cw):
    b, s, width = q3d.shape
    n_heads = width // V_HEAD_DIM
    lam_spec = pl.BlockSpec((1, HEAD_DIM), lambda i, h, j, *_: (0, 0))
    grid_spec = pltpu.PrefetchScalarGridSpec(
        num_scalar_prefetch=1,
        grid=(b, n_heads, s // tq),
        in_specs=[
            pl.BlockSpec((None, tq, V_HEAD_DIM), lambda i, h, j, *_: (i, j, h)),
            pl.BlockSpec((None, s, V_HEAD_DIM), lambda i, h, j, *_: (i, 0, h)),
            pl.BlockSpec((None, V_HEAD_DIM, s), lambda i, h, j, *_: (i, h, 0)),
            lam_spec, lam_spec, lam_spec, lam_spec,
            pl.BlockSpec((V_HEAD_DIM, 1), lambda i, h, j, *_: (0, 0)),
        ],
        out_specs=pl.BlockSpec((None, tq, V_HEAD_DIM), lambda i, h, j, *_: (i, j, h)),
        scratch_shapes=[
            pltpu.VMEM((s, 2 * LANES), BF16),
            pltpu.VMEM((2 * tq, 2 * LANES), BF16),
            pltpu.VMEM((1, 2 * tq), F32),
            pltpu.VMEM((1, 2 * tq), F32),
            pltpu.VMEM((V_HEAD_DIM, 2 * tq), F32),
            pltpu.VMEM((tq, cw), F32),
        ],
    )
    return pl.pallas_call(
        functools.partial(_attn_kernel, tq=tq, cw=cw, lam_init=lam_init),
        grid_spec=grid_spec,
        out_shape=jax.ShapeDtypeStruct((b, s, width), BF16),
        compiler_params=_cparams(3, VMEM_LIMIT),
        name="diff_attention",
    )(slopes, q3d, k3d, vt3d, lq1.reshape(1, -1), lk1.reshape(1, -1), lq2.reshape(1, -1),
      lk2.reshape(1, -1), subln_g.reshape(-1, 1))


def _mix_kernel(x_ref, attn_ref, conv_ref, sga_ref, sgc_ref, wa_ref, wc_ref, wo_ref, gf_ref, wq_ref,
                x1_ref, h2_ref, qp_ref):
    a = jnp.dot(attn_ref[...], wa_ref[...], preferred_element_type=F32)
    c = jnp.dot(conv_ref[...], wc_ref[...], preferred_element_type=F32)
    mixed = sga_ref[...].astype(F32) * a + sgc_ref[...].astype(F32) * c
    x1 = x_ref[...] + jnp.dot(mixed.astype(BF16), wo_ref[...], preferred_element_type=F32)
    ms = jnp.mean(x1 * x1, axis=-1, keepdims=True)
    h2 = (x1 * lax.rsqrt(ms + EPS)) * gf_ref[...]
    for s in range(x1_ref.shape[1]):
        x1_ref[:, s, :] = x1[:, s * LANES:(s + 1) * LANES]
        h2_ref[:, s, :] = h2[:, s * LANES:(s + 1) * LANES]
    qp_ref[...] = jnp.dot(h2.astype(BF16), wq_ref[...], preferred_element_type=F32).astype(qp_ref.dtype)


def _mix(x2d, attn2d, conv2d, sg, wa, wc, wo, gf, wq, tm):
    n, d = x2d.shape
    qw = wq.shape[1]
    row = lambda width: pl.BlockSpec((tm, width), lambda i: (i, 0))
    tiled = pl.BlockSpec((tm, d // LANES, LANES), lambda i: (i, 0, 0))
    return pl.pallas_call(
        _mix_kernel,
        grid=(n // tm,),
        in_specs=[row(d), row(d), row(d),
                  pl.BlockSpec((None, tm, d), lambda i: (0, i, 0)),
                  pl.BlockSpec((None, tm, d), lambda i: (1, i, 0)),
                  _const_spec((d, d)), _const_spec((d, d)), _const_spec((d, d)),
                  _const_spec((1, d)), _const_spec((d, qw))],
        out_specs=[tiled, tiled, row(qw)],
        out_shape=[jax.ShapeDtypeStruct((n, d // LANES, LANES), F32),
                   jax.ShapeDtypeStruct((n, d // LANES, LANES), F32),
                   jax.ShapeDtypeStruct((n, qw), BF16)],
        compiler_params=_cparams(1, VMEM_LIMIT),
        name="mix_residual",
    )(x2d, attn2d, conv2d, sg, sg, wa, wc, wo, gf.reshape(1, d), wq)


def _top_rounds(c, rowid, payload, n_rounds):
    vals, rids, pays = [], [], []
    for _ in range(n_rounds):
        m = jnp.max(c, axis=0, keepdims=True)
        rid = jnp.min(jnp.where(c == m, rowid, 1e9), axis=0, keepdims=True)
        hit = rowid == rid
        vals.append(m)
        rids.append(rid)
        if payload is not None:
            pays.append(jnp.sum(jnp.where(hit, payload, 0.0), axis=0, keepdims=True))
        c = jnp.where(hit, NEG, c)
    return vals, rids, pays


def _stack_rows(rows):
    n, tt = len(rows), rows[0].shape[1]
    rid = lax.broadcasted_iota(I32, (n, tt), 0)
    out = jnp.zeros((n, tt), rows[0].dtype)
    for r, row in enumerate(rows):
        out = jnp.where(rid == r, row, out)
    return out


def _candidates(sv, si):
    assert PEER_TOPK == 2 * SUBLANES
    tt = sv[0].shape[1]
    sub = lax.broadcasted_iota(I32, (SUBLANES, tt), 0)
    subf = sub.astype(F32)
    vals, ids, exps = [], [], []
    for b0 in (0, SUBLANES):
        vals.append(sv[0][0:1, :] + sv[1][b0:b0 + SUBLANES, :])
        ids.append(subf + float(b0))
        exps.append(si[0][0:1, :] * float(N_KEYS) + si[1][b0:b0 + SUBLANES, :])
    for a in range(1, SUBLANES):
        v = sv[0][a:a + 1, :] + sv[1][0:SUBLANES, :]
        vals.append(jnp.where(sub < PEER_TOPK // (a + 1), v, NEG))
        ids.append(subf + float(a * PEER_TOPK))
        exps.append(si[0][a:a + 1, :] * float(N_KEYS) + si[1][0:SUBLANES, :])
    vals.append(sv[0][SUBLANES:, :] + sv[1][0:1, :])
    ids.append((subf + float(SUBLANES)) * float(PEER_TOPK))
    exps.append(si[0][SUBLANES:, :] * float(N_KEYS) + si[1][0:1, :])
    cat = lambda xs: jnp.concatenate(xs, axis=0)
    return cat(vals), cat(ids), cat(exps)


def _topk_kernel(qp_ref, sk_ref, eidx_ref, g_ref, *, row_words):
    tt = qp_ref.shape[0]
    key_id = lax.broadcasted_iota(I32, (N_KEYS, tt), 0).astype(F32)
    e_rows, g_rows = [], []
    for h in range(PEER_HEADS):
        sv, si = [], []
        for p in range(2):
            hp = 2 * h + p
            q = qp_ref[:, hp * D_HALF:(hp + 1) * D_HALF]
            s = lax.dot_general(sk_ref[hp], q, _NT, preferred_element_type=F32)
            vals, rids, _ = _top_rounds(s, key_id, None, PEER_TOPK)
            sv.append(_stack_rows(vals))
            si.append(_stack_rows(rids))
        cand, cand_id, cidx = _candidates(sv, si)
        tv, _, te = _top_rounds(cand, cand_id, cidx, PEER_TOPK)
        tv = _stack_rows(tv)
        ex = jnp.exp(tv - tv[0:1, :])
        g_rows.append(ex / jnp.sum(ex, axis=0, keepdims=True))
        e_rows.append(_stack_rows(te))
    eidx_ref[...] = (jnp.concatenate(e_rows, axis=0).T * float(row_words)).astype(I32)
    g_ref[...] = jnp.concatenate(g_rows, axis=0).T


def _peer_topk(qp, sub_keys, row_words, tt):
    n, qw = qp.shape
    nk = PEER_HEADS * PEER_TOPK
    return pl.pallas_call(
        functools.partial(_topk_kernel, row_words=row_words),
        grid=(n // tt,),
        in_specs=[pl.BlockSpec((tt, qw), lambda i: (i, 0)),
                  _const_spec(sub_keys.shape)],
        out_specs=[pl.BlockSpec((tt, nk), lambda i: (i, 0)),
                   pl.BlockSpec((tt, nk), lambda i: (i, 0))],
        out_shape=[jax.ShapeDtypeStruct((n, nk), I32),
                   jax.ShapeDtypeStruct((n, nk), F32)],
        compiler_params=_cparams(1, VMEM_LIMIT),
        name="peer_topk",
    )(qp, sub_keys)


def _pack_kernel(t_ref, o_ref, *, words):
    rows = t_ref.shape[0]
    for j in range(words):
        lo = t_ref[:, (2 * j) * LANES:(2 * j + 1) * LANES].astype(BF16).astype(F32)
        hi = t_ref[:, (2 * j + 1) * LANES:(2 * j + 2) * LANES].astype(BF16).astype(F32)
        w = (pltpu.bitcast(lo, jnp.uint32) >> 16) | (pltpu.bitcast(hi, jnp.uint32) & jnp.uint32(0xFFFF0000))
        o_ref[pl.ds(j, rows, stride=words), :] = pltpu.bitcast(w, I32)


def _pack_table(t, rows=512):
    e, d = t.shape
    words = d // (2 * LANES)
    rows = _tile(e, rows)
    return pl.pallas_call(
        functools.partial(_pack_kernel, words=words),
        grid=(e // rows,),
        in_specs=[pl.BlockSpec((rows, d), lambda i: (i, 0))],
        out_specs=pl.BlockSpec((rows * words, LANES), lambda i: (i, 0)),
        out_shape=jax.ShapeDtypeStruct((e * words, LANES), I32),
        compiler_params=_cparams(1),
        name="pack_table",
    )(t)


def _gather_rows(idx_row, tab_ref, gbuf_ref, n_sel, words):
    for k in range(n_sel):
        r = pl.multiple_of(idx_row[k], words)
        gbuf_ref[pl.ds(k * words, words), :] = tab_ref[pl.ds(r, words), :]


def _diag_mask(rows, ds, n_sel):
    col = lax.broadcasted_iota(I32, (rows, ds * n_sel), 1)
    row = lax.broadcasted_iota(I32, (rows, ds * n_sel), 0)
    return (col % ds) == (row % ds)


def _gelu(a):
    return 0.5 * a * (1.0 + lax.erf(a * (1.0 / math.sqrt(2.0))))


def _split_bf16(x):
    hi = x.astype(BF16)
    return hi, (x - hi.astype(F32)).astype(BF16)


def _peer_u_kernel(eidx_ref, h_ref, g_ref, u_ref, sel_ref, w_ref, gbuf_ref, dbuf_ref, *, unroll):
    tt, ds, _ = h_ref.shape
    n_sel = g_ref.shape[1]
    words = ds // 2
    mask = _diag_mask(ds, ds, n_sel)

    def group(i, carry):
        for j in range(unroll):
            _gather_rows(eidx_ref.at[i * unroll + j], u_ref, gbuf_ref.at[j], n_sel, words)
        for j in range(unroll):
            t = i * unroll + j
            g2 = pltpu.bitcast(gbuf_ref[j], BF16)
            hs = h_ref[t].astype(BF16)
            r = lax.dot_general(hs, g2, _NT, preferred_element_type=F32)
            dbuf_ref[pl.ds(t, 1), :] = jnp.sum(jnp.where(mask, r, 0.0), axis=0, keepdims=True)
        return carry

    lax.fori_loop(0, tt // unroll, group, 0)
    sel = sel_ref[...]
    d_hi, d_lo = _split_bf16(dbuf_ref[...])
    a = (jnp.dot(d_hi, sel, preferred_element_type=F32)
         + jnp.dot(d_lo, sel, preferred_element_type=F32))
    w_ref[...] = g_ref[...] * _gelu(a)


def _peer_v_kernel(eidx_ref, w_ref, x1_ref, v_ref, selt_ref, o_ref, gbuf_ref, wrep_ref, *, unroll):
    tt, ds, _ = x1_ref.shape
    n_sel = w_ref.shape[1]
    words = ds // 2
    mask = _diag_mask(ds, ds, n_sel)
    selt = selt_ref[...]
    w_hi, w_lo = _split_bf16(w_ref[...])
    wrep_ref[...] = (jnp.dot(w_hi, selt, preferred_element_type=F32)
                     + jnp.dot(w_lo, selt, preferred_element_type=F32))

    def group(i, carry):
        for j in range(unroll):
            _gather_rows(eidx_ref.at[i * unroll + j], v_ref, gbuf_ref.at[j], n_sel, words)
        for j in range(unroll):
            t = i * unroll + j
            g2 = pltpu.bitcast(gbuf_ref[j], BF16)
            wr = jnp.broadcast_to(wrep_ref[pl.ds(t, 1), :], mask.shape)
            wm_hi, wm_lo = _split_bf16(jnp.where(mask, wr, 0.0))
            o = (jnp.dot(wm_hi, g2, preferred_element_type=F32)
                 + jnp.dot(wm_lo, g2, preferred_element_type=F32))
            o_ref[t] = x1_ref[t] + o
        return carry

    lax.fori_loop(0, tt // unroll, group, 0)


def _sel_matrix(ds, n_sel):
    grp = np.arange(ds * n_sel) // ds
    return (grp[:, None] == np.arange(n_sel)[None, :]).astype(np.float32)


PEER_UNROLL = 32


def _peer_u(eidx, h3, g, u_packed, tt):
    n, ds, _ = h3.shape
    n_sel = g.shape[1]
    words = ds // 2
    sel = jnp.asarray(_sel_matrix(ds, n_sel), BF16)
    return pl.pallas_call(
        functools.partial(_peer_u_kernel, unroll=PEER_UNROLL),
        grid=(n // tt,),
        in_specs=[pl.BlockSpec((tt, n_sel), lambda i: (i, 0), memory_space=pltpu.SMEM),
                  pl.BlockSpec((tt, ds, LANES), lambda i: (i, 0, 0)),
                  pl.BlockSpec((tt, n_sel), lambda i: (i, 0)),
                  _const_spec(u_packed.shape),
                  _const_spec(sel.shape)],
        out_specs=pl.BlockSpec((tt, n_sel), lambda i: (i, 0)),
        out_shape=jax.ShapeDtypeStruct((n, n_sel), F32),
        scratch_shapes=[pltpu.VMEM((PEER_UNROLL, n_sel * words, LANES), I32),
                        pltpu.VMEM((tt, n_sel * ds), F32)],
        compiler_params=_cparams(1, VMEM_LIMIT),
        name="peer_u",
    )(eidx, h3, g, u_packed, sel)


def _peer_v(eidx, w, x1_3, v_packed, tt):
    n, ds, _ = x1_3.shape
    n_sel = w.shape[1]
    words = ds // 2
    selt = jnp.asarray(_sel_matrix(ds, n_sel).T, BF16)
    return pl.pallas_call(
        functools.partial(_peer_v_kernel, unroll=PEER_UNROLL),
        grid=(n // tt,),
        in_specs=[pl.BlockSpec((tt, n_sel), lambda i: (i, 0), memory_space=pltpu.SMEM),
                  pl.BlockSpec((tt, n_sel), lambda i: (i, 0)),
                  pl.BlockSpec((tt, ds, LANES), lambda i: (i, 0, 0)),
                  _const_spec(v_packed.shape),
                  _const_spec(selt.shape)],
        out_specs=pl.BlockSpec((tt, ds, LANES), lambda i: (i, 0, 0)),
        out_shape=jax.ShapeDtypeStruct((n, ds, LANES), F32),
        scratch_shapes=[pltpu.VMEM((PEER_UNROLL, n_sel * words, LANES), I32),
                        pltpu.VMEM((tt, n_sel * ds), F32)],
        compiler_params=_cparams(1, VMEM_LIMIT),
        name="peer_v",
    )(eidx, w, x1_3, v_packed, selt)


def _tile(n, pref):
    t = min(n, pref)
    while n % t:
        t //= 2
    return t


def _layer(x, norm_mix_g, w_in, q_norm_g, k_norm_g, lq1, lk1, lq2, lk2, subln_g, w_attn_proj,
           conv_w, w_conv_proj, w_out, norm_ffn_g, w_query, sub_keys, peer_u, peer_v, lam_init):
    b, s, d = x.shape
    n = b * s
    n_heads = d // V_HEAD_DIM
    qk_w = n_heads * 2 * HEAD_DIM
    x2d = x.reshape(n, d)

    wb = w_in.astype(BF16)
    o = 0
    w_qk = jnp.stack([wb[:, 0:qk_w], wb[:, qk_w:2 * qk_w]])
    o = 2 * qk_w
    w_vt = wb[:, o:o + d].T
    o += d
    w_c = wb[:, o:o + 3 * d]
    o += 3 * d
    w_g = jnp.stack([wb[:, o:o + d], wb[:, o + d:o + 2 * d]])
    reps = qk_w // HEAD_DIM
    gains = jnp.stack([jnp.tile(q_norm_g.astype(F32), reps) * (HEAD_DIM ** -0.5),
                       jnp.tile(k_norm_g.astype(F32), reps)]).reshape(2, 1, qk_w)
    slopes = jnp.asarray(np.array([2.0 ** (-8.0 * (i + 1) / n_heads) for i in range(n_heads)],
                                  dtype=np.float32))

    h = _rmsnorm(x2d, norm_mix_g.astype(F32), _tile(n, 1024))
    h3d = h.reshape(b, s, d)
    qk = _qk_proj(h, w_qk, gains, _tile(n, 1024))
    vt = _vt_proj(h3d, w_vt, _tile(s, 512))
    sg = _gate_proj(h, w_g, _tile(n, 1024))
    convb = _conv_branch(h3d, w_c, conv_w.astype(F32), _tile(s, 512))
    attn = _diff_attention(qk[0].reshape(b, s, qk_w), qk[1].reshape(b, s, qk_w), vt, slopes,
                           lq1.astype(F32), lk1.astype(F32), lq2.astype(F32), lk2.astype(F32),
                           subln_g.astype(F32), lam_init, _tile(s, 512), MXU_COL)
    x1, h2, qp = _mix(x2d, attn.reshape(n, d), convb.reshape(n, d), sg,
                      w_attn_proj.astype(BF16), w_conv_proj.astype(BF16), w_out.astype(BF16),
                      norm_ffn_g.astype(F32), w_query.astype(BF16), _tile(n, 512))

    skb = sub_keys.astype(BF16).reshape(PEER_HEADS * 2, N_KEYS, D_HALF)
    ds = d // LANES
    eidx, g = _peer_topk(qp, skb, ds // 2, _tile(n, 256))
    tt = _tile(n, 128)
    w = _peer_u(eidx, h2, g, _pack_table(peer_u), tt)
    out = _peer_v(eidx, w, x1, _pack_table(peer_v), tt)
    return out.reshape(b, s, d)


def kernel(x, norm_mix_g, w_in, q_norm_g, k_norm_g, lambda_q1, lambda_k1, lambda_q2, lambda_k2, subln_g, w_attn_proj, conv_w, w_conv_proj, w_out, norm_ffn_g, peer_w_query, peer_sub_keys, peer_u, peer_v):
    depth = w_in.shape[0]
    for l in range(depth):
        lam_init = 0.8 - 0.6 * math.exp(-0.3 * l)
        x = _layer(x, norm_mix_g[l], w_in[l], q_norm_g[l], k_norm_g[l], lambda_q1[l], lambda_k1[l],
                   lambda_q2[l], lambda_k2[l], subln_g[l], w_attn_proj[l], conv_w[l], w_conv_proj[l],
                   w_out[l], norm_ffn_g[l], peer_w_query[l], peer_sub_keys[l], peer_u[l], peer_v[l],
                   lam_init)
    return x
```

```python
import functools
import math

import numpy as np
import jax
import jax.numpy as jnp
from jax import lax
from jax.experimental import pallas as pl
from jax.experimental.pallas import tpu as pltpu

F32 = jnp.float32
BF16 = jnp.bfloat16
I32 = jnp.int32

EPS = 1e-6
HEAD_DIM = 64
V_HEAD_DIM = 2 * HEAD_DIM
CONV_K = 3
PEER_HEADS = 8
N_KEYS = 128
PEER_TOPK = 16
D_HALF = 128
LANES = 128
SUBLANES = 8
MXU_COL = 256
NEG = -1e30
VMEM_LIMIT = 56 * 1024 * 1024

_NT = (((1,), (1,)), ((), ()))


def _cparams(n_axes, vmem=None):
    return pltpu.CompilerParams(
        dimension_semantics=("arbitrary",) * n_axes,
        vmem_limit_bytes=vmem)


def _const_spec(shape):
    nd = len(shape)
    return pl.BlockSpec(shape, lambda *_: (0,) * nd, pipeline_mode=pl.Buffered(1))


def _rmsnorm_kernel(x_ref, g_ref, o_ref):
    x = x_ref[...]
    ms = jnp.mean(x * x, axis=-1, keepdims=True)
    o_ref[...] = ((x * lax.rsqrt(ms + EPS)) * g_ref[...]).astype(o_ref.dtype)


def _rmsnorm(x2d, g, tm):
    n, d = x2d.shape
    return pl.pallas_call(
        _rmsnorm_kernel,
        grid=(n // tm,),
        in_specs=[pl.BlockSpec((tm, d), lambda i: (i, 0)), _const_spec((1, d))],
        out_specs=pl.BlockSpec((tm, d), lambda i: (i, 0)),
        out_shape=jax.ShapeDtypeStruct((n, d), BF16),
        compiler_params=_cparams(1),
        name="rmsnorm_in",
    )(x2d, g.reshape(1, d))


def _qk_kernel(h_ref, w_ref, g_ref, bd_ref, o_ref):
    p = jnp.dot(h_ref[...], w_ref[...], preferred_element_type=F32)
    sq = p * p
    hi = sq.astype(BF16)
    lo = (sq - hi.astype(F32)).astype(BF16)
    bd = bd_ref[...]
    parts = []
    for c in range(p.shape[1] // MXU_COL):
        sl = slice(c * MXU_COL, (c + 1) * MXU_COL)
        parts.append(jnp.dot(hi[:, sl], bd, preferred_element_type=F32)
                     + jnp.dot(lo[:, sl], bd, preferred_element_type=F32))
    ss = jnp.concatenate(parts, axis=1)
    y = p * lax.rsqrt(ss * (1.0 / HEAD_DIM) + EPS)
    o_ref[...] = (y * g_ref[...]).astype(o_ref.dtype)


def _qk_proj(h2d, w_qk, gains, tm):
    n, d = h2d.shape
    width = w_qk.shape[2]
    grp = np.arange(MXU_COL) // HEAD_DIM
    bd = jnp.asarray((grp[:, None] == grp[None, :]).astype(np.float32), BF16)
    return pl.pallas_call(
        _qk_kernel,
        grid=(2, n // tm),
        in_specs=[
            pl.BlockSpec((tm, d), lambda g, i: (i, 0)),
            pl.BlockSpec((None, d, width), lambda g, i: (g, 0, 0)),
            pl.BlockSpec((None, 1, width), lambda g, i: (g, 0, 0)),
            _const_spec((MXU_COL, MXU_COL)),
        ],
        out_specs=pl.BlockSpec((None, tm, width), lambda g, i: (g, i, 0)),
        out_shape=jax.ShapeDtypeStruct((2, n, width), BF16),
        compiler_params=_cparams(2, VMEM_LIMIT),
        name="qk_proj",
    )(h2d, w_qk, gains, bd)


def _vt_kernel(h_ref, wt_ref, o_ref):
    o_ref[...] = lax.dot_general(wt_ref[...], h_ref[...], _NT,
                                 preferred_element_type=F32).astype(o_ref.dtype)


def _vt_proj(h3d, w_vt, tm):
    b, s, d = h3d.shape
    dout = w_vt.shape[0]
    return pl.pallas_call(
        _vt_kernel,
        grid=(b, s // tm),
        in_specs=[pl.BlockSpec((None, tm, d), lambda i, j: (i, j, 0)),
                  _const_spec((dout, d))],
        out_specs=pl.BlockSpec((None, dout, tm), lambda i, j: (i, 0, j)),
        out_shape=jax.ShapeDtypeStruct((b, dout, s), BF16),
        compiler_params=_cparams(2, VMEM_LIMIT),
        name="vt_proj",
    )(h3d, w_vt)


def _gate_kernel(h_ref, w_ref, o_ref):
    p = jnp.dot(h_ref[...], w_ref[...], preferred_element_type=F32)
    o_ref[...] = (1.0 / (1.0 + jnp.exp(-p))).astype(o_ref.dtype)


def _gate_proj(h2d, w_g, tm):
    n, d = h2d.shape
    width = w_g.shape[2]
    return pl.pallas_call(
        _gate_kernel,
        grid=(2, n // tm),
        in_specs=[pl.BlockSpec((tm, d), lambda g, i: (i, 0)),
                  pl.BlockSpec((None, d, width), lambda g, i: (g, 0, 0))],
        out_specs=pl.BlockSpec((None, tm, width), lambda g, i: (g, i, 0)),
        out_shape=jax.ShapeDtypeStruct((2, n, width), BF16),
        compiler_params=_cparams(2, VMEM_LIMIT),
        name="gate_proj",
    )(h2d, w_g)


def _conv_kernel(h_ref, w_ref, cw_ref, o_ref, carry_ref):
    @pl.when(pl.program_id(1) == 0)
    def _():
        carry_ref[...] = jnp.zeros_like(carry_ref)

    width = o_ref.shape[1]
    tm = o_ref.shape[0]
    p = jnp.dot(h_ref[...], w_ref[...], preferred_element_type=F32)
    y = p[:, width:2 * width] * p[:, 2 * width:]
    rows = lax.broadcasted_iota(I32, y.shape, 0)
    prev2 = carry_ref[SUBLANES - 2:SUBLANES - 1, :]
    prev1 = carry_ref[SUBLANES - 1:SUBLANES, :]
    y1 = jnp.where(rows == 0, prev1, pltpu.roll(y, 1, 0))
    y2 = jnp.where(rows == 0, prev2, jnp.where(rows == 1, prev1, pltpu.roll(y, 2, 0)))
    cw = cw_ref[...]
    z = cw[0:1, :] * y2 + cw[1:2, :] * y1 + cw[2:3, :] * y
    o_ref[...] = (p[:, :width] * z).astype(o_ref.dtype)
    carry_ref[...] = y[tm - SUBLANES:, :]


def _conv_branch(h3d, w_c, conv_w, tm):
    b, s, d = h3d.shape
    width = conv_w.shape[1]
    return pl.pallas_call(
        _conv_kernel,
        grid=(b, s // tm),
        in_specs=[pl.BlockSpec((None, tm, d), lambda i, j: (i, j, 0)),
                  _const_spec((d, 3 * width)),
                  _const_spec((CONV_K, width))],
        out_specs=pl.BlockSpec((None, tm, width), lambda i, j: (i, j, 0)),
        out_shape=jax.ShapeDtypeStruct((b, s, width), BF16),
        scratch_shapes=[pltpu.VMEM((SUBLANES, width), F32)],
        compiler_params=_cparams(2, VMEM_LIMIT),
        name="conv_branch",
    )(h3d, w_c, conv_w)


def _attn_kernel(slopes_ref, q_ref, k_ref, vt_ref, lq1_ref, lk1_ref, lq2_ref, lk2_ref, sg_ref,
                 o_ref, kaug_ref, qaug_ref, m_ref, l_ref, acc_ref, s0_ref, *, tq, cw, lam_init):
    h = pl.program_id(1)
    qi = pl.program_id(2)
    s_len = k_ref.shape[0]
    slope = slopes_ref[h]

    @pl.when(qi == 0)
    def _():
        kpos = lax.broadcasted_iota(I32, (s_len, LANES), 0).astype(F32) * slope
        lane = lax.broadcasted_iota(I32, (s_len, LANES), 1)
        b_hi = kpos.astype(BF16).astype(F32)
        b_lo = kpos - b_hi
        extra = jnp.where(lane == 0, b_hi, jnp.where(lane == 1, b_lo, jnp.where(lane < 4, 1.0, 0.0)))
        kaug_ref[:, :LANES] = k_ref[...]
        kaug_ref[:, LANES:] = extra.astype(BF16)

    q = q_ref[...].astype(F32)
    lane = lax.broadcasted_iota(I32, (tq, LANES), 1)
    c_full = -((jnp.zeros((tq, LANES), I32) + qi * tq).astype(F32) * slope)
    c_hi = c_full.astype(BF16).astype(F32)
    c_lo = c_full - c_hi
    qextra = jnp.where(lane < 2, 1.0, jnp.where(lane == 2, c_hi, jnp.where(lane == 3, c_lo, 0.0)))
    qextra = qextra.astype(BF16)
    qaug_ref[:tq, :LANES] = jnp.where(lane < HEAD_DIM, q, 0.0).astype(BF16)
    qaug_ref[tq:, :LANES] = jnp.where(lane >= HEAD_DIM, q, 0.0).astype(BF16)
    qaug_ref[:tq, LANES:] = qextra
    qaug_ref[tq:, LANES:] = qextra

    m_ref[...] = jnp.full(m_ref.shape, NEG, F32)
    l_ref[...] = jnp.zeros(l_ref.shape, F32)
    acc_ref[...] = jnp.zeros(acc_ref.shape, F32)

    def scores(kt, c, masked):
        lo = c * cw
        q_lo = lo % tq
        kn = q_lo + cw if masked else tq
        k0 = pl.multiple_of(kt * tq, tq)
        s = lax.dot_general(kaug_ref[pl.ds(k0, kn), :], qaug_ref[lo:lo + cw, :], _NT,
                            preferred_element_type=F32)
        if masked:
            kk = lax.broadcasted_iota(I32, s.shape, 0)
            qq = lax.broadcasted_iota(I32, s.shape, 1) + q_lo
            s = jnp.where(kk <= qq, s, NEG)
        return s

    def update(kt, c, s):
        lo = c * cw
        kn = s.shape[0]
        k0 = pl.multiple_of(kt * tq, tq)
        m_old = m_ref[:, lo:lo + cw]
        m_new = jnp.maximum(m_old, jnp.max(s, axis=0, keepdims=True))
        alpha = jnp.exp(m_old - m_new)
        p = jnp.exp(s - m_new)
        l_ref[:, lo:lo + cw] = alpha * l_ref[:, lo:lo + cw] + jnp.sum(p, axis=0, keepdims=True)
        pv = jnp.dot(vt_ref[:, pl.ds(k0, kn)], p.astype(BF16), preferred_element_type=F32)
        acc_ref[:, lo:lo + cw] = alpha * acc_ref[:, lo:lo + cw] + pv
        m_ref[:, lo:lo + cw] = m_new

    def step(kt, masked):
        n_chunks = 2 * tq // cw
        if masked:
            s_next = s0_ref[:cw, :]
            kk = lax.broadcasted_iota(I32, s_next.shape, 0)
            qq = lax.broadcasted_iota(I32, s_next.shape, 1)
            s_next = jnp.where(kk <= qq, s_next, NEG)
        else:
            s_next = s0_ref[...]
        for c in range(n_chunks):
            s_cur = s_next
            if c + 1 < n_chunks:
                s_next = scores(kt, c + 1, masked)
            update(kt, c, s_cur)
        if not masked:
            s0_ref[...] = scores(kt + 1, 0, False)

    def body(kt, carry):
        step(kt, False)
        return carry

    s0_ref[...] = scores(0, 0, False)
    lax.fori_loop(0, qi, body, 0)
    step(qi, True)

    lam = (jnp.exp(jnp.sum(lq1_ref[...] * lk1_ref[...], axis=1, keepdims=True))
           - jnp.exp(jnp.sum(lq2_ref[...] * lk2_ref[...], axis=1, keepdims=True))
           + lam_init)
    o = acc_ref[...] / l_ref[...]
    a = o[:, :tq] - lam * o[:, tq:]
    ms = jnp.mean(a * a, axis=0, keepdims=True)
    a = (a * lax.rsqrt(ms + EPS)) * sg_ref[...]
    a = a * (1.0 - lam_init)
    o_ref[...] = a.T.astype(o_ref.dtype)


def _diff_attention(q3d, k3d, vt3d, slopes, lq1, lk1, lq2, lk2, subln_g, lam_init, tq, cw):
    b, s, width = q3d.shape
    n_heads = width // V_HEAD_DIM
    lam_spec = pl.BlockSpec((1, HEAD_DIM), lambda i, h, j, *_: (0, 0))
    grid_spec = pltpu.PrefetchScalarGridSpec(
        num_scalar_prefetch=1,
        grid=(b, n_heads, s // tq),
        in_specs=[
            pl.BlockSpec((None, tq, V_HEAD_DIM), lambda i, h, j, *_: (i, j, h)),
            pl.BlockSpec((None, s, V_HEAD_DIM), lambda i, h, j, *_: (i, 0, h)),
            pl.BlockSpec((None, V_HEAD_DIM, s), lambda i, h, j, *_: (i, h, 0)),
            lam_spec, lam_spec, lam_spec, lam_spec,
            pl.BlockSpec((V_HEAD_DIM, 1), lambda i, h, j, *_: (0, 0)),
        ],
        out_specs=pl.BlockSpec((None, tq, V_HEAD_DIM), lambda i, h, j, *_: (i, j, h)),
        scratch_shapes=[
            pltpu.VMEM((s, 2 * LANES), BF16),
            pltpu.VMEM((2 * tq, 2 * LANES), BF16),
            pltpu.VMEM((1, 2 * tq), F32),
            pltpu.VMEM((1, 2 * tq), F32),
            pltpu.VMEM((V_HEAD_DIM, 2 * tq), F32),
            pltpu.VMEM((tq, cw), F32),
        ],
    )
    return pl.pallas_call(
        functools.partial(_attn_kernel, tq=tq, cw=cw, lam_init=lam_init),
        grid_spec=grid_spec,
        out_shape=jax.ShapeDtypeStruct((b, s, width), BF16),
        compiler_params=_cparams(3, VMEM_LIMIT),
        name="diff_attention",
    )(slopes, q3d, k3d, vt3d, lq1.reshape(1, -1), lk1.reshape(1, -1), lq2.reshape(1, -1),
      lk2.reshape(1, -1), subln_g.reshape(-1, 1))


def _mix_kernel(x_ref, attn_ref, conv_ref, sga_ref, sgc_ref, wa_ref, wc_ref, wo_ref, gf_ref, wq_ref,
                x1_ref, h2_ref, qp_ref):
    a = jnp.dot(attn_ref[...], wa_ref[...], preferred_element_type=F32)
    c = jnp.dot(conv_ref[...], wc_ref[...], preferred_element_type=F32)
    mixed = sga_ref[...].astype(F32) * a + sgc_ref[...].astype(F32) * c
    x1 = x_ref[...] + jnp.dot(mixed.astype(BF16), wo_ref[...], preferred_element_type=F32)
    ms = jnp.mean(x1 * x1, axis=-1, keepdims=True)
    h2 = (x1 * lax.rsqrt(ms + EPS)) * gf_ref[...]
    for s in range(x1_ref.shape[1]):
        x1_ref[:, s, :] = x1[:, s * LANES:(s + 1) * LANES]
        h2_ref[:, s, :] = h2[:, s * LANES:(s + 1) * LANES]
    qp_ref[...] = jnp.dot(h2.astype(BF16), wq_ref[...], preferred_element_type=F32).astype(qp_ref.dtype)


def _mix(x2d, attn2d, conv2d, sg, wa, wc, wo, gf, wq, tm):
    n, d = x2d.shape
    qw = wq.shape[1]
    row = lambda width: pl.BlockSpec((tm, width), lambda i: (i, 0))
    tiled = pl.BlockSpec((tm, d // LANES, LANES), lambda i: (i, 0, 0))
    return pl.pallas_call(
        _mix_kernel,
        grid=(n // tm,),
        in_specs=[row(d), row(d), row(d),
                  pl.BlockSpec((None, tm, d), lambda i: (0, i, 0)),
                  pl.BlockSpec((None, tm, d), lambda i: (1, i, 0)),
                  _const_spec((d, d)), _const_spec((d, d)), _const_spec((d, d)),
                  _const_spec((1, d)), _const_spec((d, qw))],
        out_specs=[tiled, tiled, row(qw)],
        out_shape=[jax.ShapeDtypeStruct((n, d // LANES, LANES), F32),
                   jax.ShapeDtypeStruct((n, d // LANES, LANES), F32),
                   jax.ShapeDtypeStruct((n, qw), BF16)],
        compiler_params=_cparams(1, VMEM_LIMIT),
        name="mix_residual",
    )(x2d, attn2d, conv2d, sg, sg, wa, wc, wo, gf.reshape(1, d), wq)


def _top_rounds(c, rowid, payload, n_rounds):
    vals, rids, pays = [], [], []
    for _ in range(n_rounds):
        m = jnp.max(c, axis=0, keepdims=True)
        rid = jnp.min(jnp.where(c == m, rowid, 1e9), axis=0, keepdims=True)
        hit = rowid == rid
        vals.append(m)
        rids.append(rid)
        if payload is not None:
            pays.append(jnp.sum(jnp.where(hit, payload, 0.0), axis=0, keepdims=True))
        c = jnp.where(hit, NEG, c)
    return vals, rids, pays


def _stack_rows(rows):
    n, tt = len(rows), rows[0].shape[1]
    rid = lax.broadcasted_iota(I32, (n, tt), 0)
    out = jnp.zeros((n, tt), rows[0].dtype)
    for r, row in enumerate(rows):
        out = jnp.where(rid == r, row, out)
    return out


def _candidates(sv, si):
    assert PEER_TOPK == 2 * SUBLANES
    tt = sv[0].shape[1]
    sub = lax.broadcasted_iota(I32, (SUBLANES, tt), 0)
    subf = sub.astype(F32)
    vals, ids, exps = [], [], []
    for b0 in (0, SUBLANES):
        vals.append(sv[0][0:1, :] + sv[1][b0:b0 + SUBLANES, :])
        ids.append(subf + float(b0))
        exps.append(si[0][0:1, :] * float(N_KEYS) + si[1][b0:b0 + SUBLANES, :])
    for a in range(1, SUBLANES):
        v = sv[0][a:a + 1, :] + sv[1][0:SUBLANES, :]
        vals.append(jnp.where(sub < PEER_TOPK // (a + 1), v, NEG))
        ids.append(subf + float(a * PEER_TOPK))
        exps.append(si[0][a:a + 1, :] * float(N_KEYS) + si[1][0:SUBLANES, :])
    vals.append(sv[0][SUBLANES:, :] + sv[1][0:1, :])
    ids.append((subf + float(SUBLANES)) * float(PEER_TOPK))
    exps.append(si[0][SUBLANES:, :] * float(N_KEYS) + si[1][0:1, :])
    cat = lambda xs: jnp.concatenate(xs, axis=0)
    return cat(vals), cat(ids), cat(exps)


def _topk_kernel(qp_ref, sk_ref, eidx_ref, g_ref, *, row_words):
    tt = qp_ref.shape[0]
    key_id = lax.broadcasted_iota(I32, (N_KEYS, tt), 0).astype(F32)
    e_rows, g_rows = [], []
    for h in range(PEER_HEADS):
        sv, si = [], []
        for p in range(2):
            hp = 2 * h + p
            q = qp_ref[:, hp * D_HALF:(hp + 1) * D_HALF]
            s = lax.dot_general(sk_ref[hp], q, _NT, preferred_element_type=F32)
            vals, rids, _ = _top_rounds(s, key_id, None, PEER_TOPK)
            sv.append(_stack_rows(vals))
            si.append(_stack_rows(rids))
        cand, cand_id, cidx = _candidates(sv, si)
        tv, _, te = _top_rounds(cand, cand_id, cidx, PEER_TOPK)
        tv = _stack_rows(tv)
        ex = jnp.exp(tv - tv[0:1, :])
        g_rows.append(ex / jnp.sum(ex, axis=0, keepdims=True))
        e_rows.append(_stack_rows(te))
    eidx_ref[...] = (jnp.concatenate(e_rows, axis=0).T * float(row_words)).astype(I32)
    g_ref[...] = jnp.concatenate(g_rows, axis=0).T


def _peer_topk(qp, sub_keys, row_words, tt):
    n, qw = qp.shape
    nk = PEER_HEADS * PEER_TOPK
    return pl.pallas_call(
        functools.partial(_topk_kernel, row_words=row_words),
        grid=(n // tt,),
        in_specs=[pl.BlockSpec((tt, qw), lambda i: (i, 0)),
                  _const_spec(sub_keys.shape)],
        out_specs=[pl.BlockSpec((tt, nk), lambda i: (i, 0)),
                   pl.BlockSpec((tt, nk), lambda i: (i, 0))],
        out_shape=[jax.ShapeDtypeStruct((n, nk), I32),
                   jax.ShapeDtypeStruct((n, nk), F32)],
        compiler_params=_cparams(1, VMEM_LIMIT),
        name="peer_topk",
    )(qp, sub_keys)


def _pack_kernel(t_ref, o_ref, *, words):
    rows = t_ref.shape[0]
    for j in range(words):
        lo = t_ref[:, (2 * j) * LANES:(2 * j + 1) * LANES].astype(BF16).astype(F32)
        hi = t_ref[:, (2 * j + 1) * LANES:(2 * j + 2) * LANES].astype(BF16).astype(F32)
        w = (pltpu.bitcast(lo, jnp.uint32) >> 16) | (pltpu.bitcast(hi, jnp.uint32) & jnp.uint32(0xFFFF0000))
        o_ref[pl.ds(j, rows, stride=words), :] = pltpu.bitcast(w, I32)


def _pack_table(t, rows=512):
    e, d = t.shape
    words = d // (2 * LANES)
    rows = _tile(e, rows)
    return pl.pallas_call(
        functools.partial(_pack_kernel, words=words),
        grid=(e // rows,),
        in_specs=[pl.BlockSpec((rows, d), lambda i: (i, 0))],
        out_specs=pl.BlockSpec((rows * words, LANES), lambda i: (i, 0)),
        out_shape=jax.ShapeDtypeStruct((e * words, LANES), I32),
        compiler_params=_cparams(1),
        name="pack_table",
    )(t)


def _gather_rows(idx_row, tab_ref, gbuf_ref, n_sel, words):
    for k in range(n_sel):
        r = pl.multiple_of(idx_row[k], words)
        gbuf_ref[pl.ds(k * words, words), :] = tab_ref[pl.ds(r, words), :]


def _diag_mask(rows, ds, n_sel):
    col = lax.broadcasted_iota(I32, (rows, ds * n_sel), 1)
    row = lax.broadcasted_iota(I32, (rows, ds * n_sel), 0)
    return (col % ds) == (row % ds)


def _gelu(a):
    return 0.5 * a * (1.0 + lax.erf(a * (1.0 / math.sqrt(2.0))))


def _split_bf16(x):
    hi = x.astype(BF16)
    return hi, (x - hi.astype(F32)).astype(BF16)


def _peer_u_kernel(eidx_ref, h_ref, g_ref, u_ref, sel_ref, w_ref, gbuf_ref, dbuf_ref, *, unroll):
    tt, ds, _ = h_ref.shape
    n_sel = g_ref.shape[1]
    words = ds // 2
    mask = _diag_mask(ds, ds, n_sel)

    def group(i, carry):
        for j in range(unroll):
            _gather_rows(eidx_ref.at[i * unroll + j], u_ref, gbuf_ref.at[j], n_sel, words)
        for j in range(unroll):
            t = i * unroll + j
            g2 = pltpu.bitcast(gbuf_ref[j], BF16)
            hs = h_ref[t].astype(BF16)
            r = lax.dot_general(hs, g2, _NT, preferred_element_type=F32)
            dbuf_ref[pl.ds(t, 1), :] = jnp.sum(jnp.where(mask, r, 0.0), axis=0, keepdims=True)
        return carry

    lax.fori_loop(0, tt // unroll, group, 0)
    sel = sel_ref[...]
    d_hi, d_lo = _split_bf16(dbuf_ref[...])
    a = (jnp.dot(d_hi, sel, preferred_element_type=F32)
         + jnp.dot(d_lo, sel, preferred_element_type=F32))
    w_ref[...] = g_ref[...] * _gelu(a)


def _peer_v_kernel(eidx_ref, w_ref, x1_ref, v_ref, selt_ref, o_ref, gbuf_ref, wrep_ref, *, unroll):
    tt, ds, _ = x1_ref.shape
    n_sel = w_ref.shape[1]
    words = ds // 2
    mask = _diag_mask(ds, ds, n_sel)
    selt = selt_ref[...]
    w_hi, w_lo = _split_bf16(w_ref[...])
    wrep_ref[...] = (jnp.dot(w_hi, selt, preferred_element_type=F32)
                     + jnp.dot(w_lo, selt, preferred_element_type=F32))

    def group(i, carry):
        for j in range(unroll):
            _gather_rows(eidx_ref.at[i * unroll + j], v_ref, gbuf_ref.at[j], n_sel, words)
        for j in range(unroll):
            t = i * unroll + j
            g2 = pltpu.bitcast(gbuf_ref[j], BF16)
            wr = jnp.broadcast_to(wrep_ref[pl.ds(t, 1), :], mask.shape)
            wm_hi, wm_lo = _split_bf16(jnp.where(mask, wr, 0.0))
            o2 = jnp.dot(jnp.concatenate([wm_hi, wm_lo], axis=0), g2, preferred_element_type=F32)
            o = o2[:ds, :] + o2[ds:, :]
            o_ref[t] = x1_ref[t] + o
        return carry

    lax.fori_loop(0, tt // unroll, group, 0)


def _sel_matrix(ds, n_sel):
    grp = np.arange(ds * n_sel) // ds
    return (grp[:, None] == np.arange(n_sel)[None, :]).astype(np.float32)


PEER_UNROLL = 32


def _peer_u(eidx, h3, g, u_packed, tt):
    n, ds, _ = h3.shape
    n_sel = g.shape[1]
    words = ds // 2
    sel = jnp.asarray(_sel_matrix(ds, n_sel), BF16)
    return pl.pallas_call(
        functools.partial(_peer_u_kernel, unroll=PEER_UNROLL),
        grid=(n // tt,),
        in_specs=[pl.BlockSpec((tt, n_sel), lambda i: (i, 0), memory_space=pltpu.SMEM),
                  pl.BlockSpec((tt, ds, LANES), lambda i: (i, 0, 0)),
                  pl.BlockSpec((tt, n_sel), lambda i: (i, 0)),
                  _const_spec(u_packed.shape),
                  _const_spec(sel.shape)],
        out_specs=pl.BlockSpec((tt, n_sel), lambda i: (i, 0)),
        out_shape=jax.ShapeDtypeStruct((n, n_sel), F32),
        scratch_shapes=[pltpu.VMEM((PEER_UNROLL, n_sel * words, LANES), I32),
                        pltpu.VMEM((tt, n_sel * ds), F32)],
        compiler_params=_cparams(1, VMEM_LIMIT),
        name="peer_u",
    )(eidx, h3, g, u_packed, sel)


def _peer_v(eidx, w, x1_3, v_packed, tt):
    n, ds, _ = x1_3.shape
    n_sel = w.shape[1]
    words = ds // 2
    selt = jnp.asarray(_sel_matrix(ds, n_sel).T, BF16)
    return pl.pallas_call(
        functools.partial(_peer_v_kernel, unroll=PEER_UNROLL),
        grid=(n // tt,),
        in_specs=[pl.BlockSpec((tt, n_sel), lambda i: (i, 0), memory_space=pltpu.SMEM),
                  pl.BlockSpec((tt, n_sel), lambda i: (i, 0)),
                  pl.BlockSpec((tt, ds, LANES), lambda i: (i, 0, 0)),
                  _const_spec(v_packed.shape),
                  _const_spec(selt.shape)],
        out_specs=pl.BlockSpec((tt, ds, LANES), lambda i: (i, 0, 0)),
        out_shape=jax.ShapeDtypeStruct((n, ds, LANES), F32),
        scratch_shapes=[pltpu.VMEM((PEER_UNROLL, n_sel * words, LANES), I32),
                        pltpu.VMEM((tt, n_sel * ds), F32)],
        compiler_params=_cparams(1, VMEM_LIMIT),
        name="peer_v",
    )(eidx, w, x1_3, v_packed, selt)


def _tile(n, pref):
    t = min(n, pref)
    while n % t:
        t //= 2
    return t


def _layer(x, norm_mix_g, w_in, q_norm_g, k_norm_g, lq1, lk1, lq2, lk2, subln_g, w_attn_proj,
           conv_w, w_conv_proj, w_out, norm_ffn_g, w_query, sub_keys, peer_u, peer_v, lam_init):
    b, s, d = x.shape
    n = b * s
    n_heads = d // V_HEAD_DIM
    qk_w = n_heads * 2 * HEAD_DIM
    x2d = x.reshape(n, d)

    wb = w_in.astype(BF16)
    o = 0
    w_qk = jnp.stack([wb[:, 0:qk_w], wb[:, qk_w:2 * qk_w]])
    o = 2 * qk_w
    w_vt = wb[:, o:o + d].T
    o += d
    w_c = wb[:, o:o + 3 * d]
    o += 3 * d
    w_g = jnp.stack([wb[:, o:o + d], wb[:, o + d:o + 2 * d]])
    reps = qk_w // HEAD_DIM
    gains = jnp.stack([jnp.tile(q_norm_g.astype(F32), reps) * (HEAD_DIM ** -0.5),
                       jnp.tile(k_norm_g.astype(F32), reps)]).reshape(2, 1, qk_w)
    slopes = jnp.asarray(np.array([2.0 ** (-8.0 * (i + 1) / n_heads) for i in range(n_heads)],
                                  dtype=np.float32))

    h = _rmsnorm(x2d, norm_mix_g.astype(F32), _tile(n, 1024))
    h3d = h.reshape(b, s, d)
    qk = _qk_proj(h, w_qk, gains, _tile(n, 1024))
    vt = _vt_proj(h3d, w_vt, _tile(s, 512))
    sg = _gate_proj(h, w_g, _tile(n, 1024))
    convb = _conv_branch(h3d, w_c, conv_w.astype(F32), _tile(s, 512))
    attn = _diff_attention(qk[0].reshape(b, s, qk_w), qk[1].reshape(b, s, qk_w), vt, slopes,
                           lq1.astype(F32), lk1.astype(F32), lq2.astype(F32), lk2.astype(F32),
                           subln_g.astype(F32), lam_init, _tile(s, 512), MXU_COL)
    x1, h2, qp = _mix(x2d, attn.reshape(n, d), convb.reshape(n, d), sg,
                      w_attn_proj.astype(BF16), w_conv_proj.astype(BF16), w_out.astype(BF16),
                      norm_ffn_g.astype(F32), w_query.astype(BF16), _tile(n, 512))

    skb = sub_keys.astype(BF16).reshape(PEER_HEADS * 2, N_KEYS, D_HALF)
    ds = d // LANES
    eidx, g = _peer_topk(qp, skb, ds // 2, _tile(n, 256))
    tt = _tile(n, 128)
    w = _peer_u(eidx, h2, g, _pack_table(peer_u), tt)
    out = _peer_v(eidx, w, x1, _pack_table(peer_v), tt)
    return out.reshape(b, s, d)


def kernel(x, norm_mix_g, w_in, q_norm_g, k_norm_g, lambda_q1, lambda_k1, lambda_q2, lambda_k2, subln_g, w_attn_proj, conv_w, w_conv_proj, w_out, norm_ffn_g, peer_w_query, peer_sub_keys, peer_u, peer_v):
    depth = w_in.shape[0]
    for l in range(depth):
        lam_init = 0.8 - 0.6 * math.exp(-0.3 * l)
        x = _layer(x, norm_mix_g[l], w_in[l], q_norm_g[l], k_norm_g[l], lambda_q1[l], lambda_k1[l],
                   lambda_q2[l], lambda_k2[l], subln_g[l], w_attn_proj[l], conv_w[l], w_conv_proj[l],
                   w_out[l], norm_ffn_g[l], peer_w_query[l], peer_sub_keys[l], peer_u[l], peer_v[l],
                   lam_init)
    return x
```
